```python
import functools
import jax, jax.numpy as jnp
from jax import lax
import numpy as np

D_MODEL = 4096
BATCH = 4
SEQ = 2048
DEPTH = 2
DEC_BATCH = 8
DEC_SEQ = 4
PAST_LEN = 16384
PAGE_SIZE = 128

D_A = D_MODEL // 4
DH_A = 64
H_A = D_A // DH_A
R_W = 64
R_A = 64
R_G = 128
A_PROJ = 3 * D_A + R_W + R_A + R_G
GN_EPS = 64e-5
D_P = D_MODEL // 4
POOL_WINDOWS = (2, 4, 8, 16)
N_POOL_GROUPS = len(POOL_WINDOWS)
D_PG = D_P // N_POOL_GROUPS
POOL_BUF = max(POOL_WINDOWS) - 1
D_C = D_MODEL - D_A - D_P
DH_C = 128
H_C = D_C // DH_C
SB_BLOCK = 128
SB_BIAS_INIT = -7.0
P_TOTAL = A_PROJ + D_P + 3 * D_C
N_KEYS = 128
N_EXPERTS = N_KEYS * N_KEYS
PEER_HEADS = 8
PEER_TOPK = 16
PEER_QDIM = 256
PEER_HALF = PEER_QDIM // 2
PEER_BLOCK = 32
NORM_EPS = 1e-6

kernel_name = 'hybrid_rwkv7_pool_stickbreak_peer_step'


def rms_norm(x, g):
    xf = x.astype(jnp.float32)
    y = xf * lax.rsqrt(jnp.mean(xf * xf, axis=-1, keepdims=True) + NORM_EPS)
    return (y * g.astype(jnp.float32)).astype(x.dtype)


def modulate(h, shift, scale):
    return h * (1 + scale[:, None, :]) + shift[:, None, :]


def token_shift(p, prev, mu):
    prev_seq = jnp.concatenate([prev[:, None, :].astype(p.dtype), p[:, :-1]], axis=1)
    return p + (prev_seq - p) * mu


def rwkv7_scan(r, w, k, v, kk, a, s0):
    def step(s, inp):
        r_t, w_t, k_t, v_t, kk_t, a_t = inp
        sa = jnp.einsum('bhvk,bhk->bhv', s, -kk_t)
        s = (s * w_t[:, :, None, :] + sa[..., None] * (kk_t * a_t)[:, :, None, :]
             + v_t[..., None] * k_t[:, :, None, :])
        return s, jnp.einsum('bhvk,bhk->bhv', s, r_t)
    xs = tuple(jnp.moveaxis(t, 1, 0) for t in (r, w, k, v, kk, a))
    s_fin, ys = lax.scan(step, s0, xs)
    return jnp.moveaxis(ys, 0, 1), s_fin


def rwkv7_mix(p_a, shift_prev, wkv0, prm):
    f32 = jnp.float32
    B, T, _ = p_a.shape
    xs = token_shift(p_a, shift_prev, prm['mu_shift'])
    o1, o2, o3 = D_A, 2 * D_A, 3 * D_A
    o4, o5 = o3 + R_W, o3 + R_W + R_A
    xr, xk, xv = xs[..., :o1], xs[..., o1:o2], xs[..., o2:o3]
    xw, xa, xg = xs[..., o3:o4], xs[..., o4:o5], xs[..., o5:]
    heads = lambda t: t.astype(f32).reshape(B, T, H_A, DH_A)
    w_log = -jax.nn.softplus(-(prm['w0'] + jnp.tanh(xw) @ prm['w_w2']).astype(f32)) - 0.5
    decay = heads(jnp.exp(-jnp.exp(w_log)))
    a = heads(jax.nn.sigmoid((prm['a0'] + xa @ prm['w_a2']).astype(f32)))
    g = (jax.nn.sigmoid(xg) @ prm['w_g2']).astype(f32)
    r, k, v = heads(xr), heads(xk), heads(xv)
    kk = k * prm['k_k'].astype(f32).reshape(H_A, DH_A)
    kk = kk / jnp.maximum(jnp.sqrt(jnp.sum(kk * kk, axis=-1, keepdims=True)), 1e-12)
    k = k * (1 + (a - 1) * prm['k_a'].astype(f32).reshape(H_A, DH_A))
    y, s_new = rwkv7_scan(r, decay, k, v, kk, a, wkv0.astype(f32))
    mean = jnp.mean(y, axis=-1, keepdims=True)
    var = jnp.mean(jnp.square(y - mean), axis=-1, keepdims=True)
    yn = ((y - mean) * lax.rsqrt(var + GN_EPS)).reshape(B, T, D_A)
    yn = yn * prm['gn_g'].astype(f32) + prm['gn_b'].astype(f32)
    bonus = (jnp.sum(r * k * prm['r_k'].astype(f32), axis=-1, keepdims=True) * v).reshape(B, T, D_A)
    out = ((yn + bonus) * g).astype(p_a.dtype)
    return out, s_new.astype(p_a.dtype), p_a[:, -1]


def pool_mix(u, u_prev, w_pool, scale):
    f32 = jnp.float32
    B, T, _ = u.shape
    n_prev = u_prev.shape[1]
    ext = jnp.concatenate([u_prev.astype(u.dtype), u], axis=1)
    cs = jnp.cumsum(ext.astype(f32), axis=1)
    cs0 = jnp.concatenate([jnp.zeros((B, 1, D_P), f32), cs], axis=1)
    hi = np.arange(n_prev + 1, n_prev + T + 1)
    groups = []
    for gi, win in enumerate(POOL_WINDOWS):
        lo = np.maximum(hi - win, 0)
        cnt = jnp.asarray(hi - lo, dtype=f32)[None, :, None]
        sl = slice(gi * D_PG, (gi + 1) * D_PG)
        groups.append((cs0[:, hi, sl] - cs0[:, lo, sl]) / cnt)
    diff = (jnp.concatenate(groups, axis=-1) - u.astype(f32)).astype(u.dtype)
    mixed = jnp.einsum('btgc,gcd->btgd', diff.reshape(B, T, N_POOL_GROUPS, D_PG), w_pool)
    return mixed.reshape(B, T, D_P) * scale, ext[:, -POOL_BUF:]


def stick_breaking(q, k, v, bias, q_pos, k_pos):
    z = (jnp.einsum('bqhd,bkhd->bhqk', q, k).astype(jnp.float32) * (DH_C ** -0.5)
         + bias.astype(jnp.float32)[None, :, None, None])
    mask = k_pos[None, :] < q_pos[:, None]
    log_1m = jnp.where(mask, jax.nn.log_sigmoid(-z), 0.0)
    suffix = lax.cumsum(log_1m, axis=3, reverse=True) - log_1m
    wts = jnp.where(mask, jnp.exp(jax.nn.log_sigmoid(z) + suffix), 0.0)
    return jnp.einsum('bhqk,bkhd->bqhd', wts.astype(v.dtype), v)


def stick_prompt(q, k, v, bias):
    T = q.shape[1]
    outs = []
    for i in range(T // SB_BLOCK):
        lo, hi = i * SB_BLOCK, (i + 1) * SB_BLOCK
        pos = jnp.arange(hi)
        outs.append(stick_breaking(q[:, lo:hi], k[:, :hi], v[:, :hi], bias, pos[lo:hi], pos))
    return jnp.concatenate(outs, axis=1)


def stick_sample(q, k, v, bias, k_past, v_past):
    past = k_past.shape[1]
    T = q.shape[1]
    k_all = jnp.concatenate([k_past.astype(k.dtype), k], axis=1)
    v_all = jnp.concatenate([v_past.astype(v.dtype), v], axis=1)
    return stick_breaking(q, k_all, v_all, bias, past + jnp.arange(T), jnp.arange(past + T))


def mixer(h, prm, shift_prev, wkv0, pool_prev, attend):
    B, T, _ = h.shape
    proj = h @ prm['w_in']
    p_a = proj[..., :A_PROJ]
    u = proj[..., A_PROJ:A_PROJ + D_P]
    qkv = proj[..., A_PROJ + D_P:].reshape(B, T, 3, H_C, DH_C)
    q_c, k_c, v_c = qkv[:, :, 0], qkv[:, :, 1], qkv[:, :, 2]
    o_a, wkv_new, shift_new = rwkv7_mix(p_a, shift_prev, wkv0, prm)
    o_b, pool_new = pool_mix(u, pool_prev, prm['w_pool'], prm['pool_scale'])
    o_c = attend(q_c, k_c, v_c, prm['sb_bias']).reshape(B, T, D_C)
    cat = jnp.concatenate([o_a, o_b.astype(o_a.dtype), o_c.astype(o_a.dtype)], axis=-1)
    return cat @ prm['w_o'], (k_c, v_c, wkv_new, shift_new, pool_new)


def peer_ffn(h, prm):
    B, T, D = h.shape
    n = B * T
    hf = h.reshape(n, D)
    q = (hf @ prm['w_pq']).reshape(n, PEER_HEADS, 2, PEER_HALF)
    s = jnp.einsum('nhpd,hpkd->nhpk', q, prm['sub_keys']).astype(jnp.float32)
    s1, i1 = lax.top_k(s[:, :, 0], PEER_TOPK)
    s2, i2 = lax.top_k(s[:, :, 1], PEER_TOPK)
    n_cand = PEER_TOPK * PEER_TOPK
    cand = (s1[..., :, None] + s2[..., None, :]).reshape(n, PEER_HEADS, n_cand)
    cidx = (i1[..., :, None] * N_KEYS + i2[..., None, :]).reshape(n, PEER_HEADS, n_cand)
    top, sel = lax.top_k(cand, PEER_TOPK)
    eidx = jnp.take_along_axis(cidx, sel, axis=-1).reshape(n, PEER_HEADS * PEER_TOPK)
    gate = jax.nn.softmax(top, axis=-1).reshape(n, PEER_HEADS * PEER_TOPK).astype(h.dtype)
    pad = (-n) % PEER_BLOCK
    hb = jnp.pad(hf, ((0, pad), (0, 0))).reshape(-1, PEER_BLOCK, D)
    eb = jnp.pad(eidx, ((0, pad), (0, 0))).reshape(-1, PEER_BLOCK, PEER_HEADS * PEER_TOPK)
    gb = jnp.pad(gate, ((0, pad), (0, 0))).reshape(-1, PEER_BLOCK, PEER_HEADS * PEER_TOPK)

    def expert_block(args):
        hx, ex, gx = args
        act = jax.nn.gelu(jnp.einsum('nd,ned->ne', hx, prm['peer_u'][ex]))
        return jnp.einsum('ne,ned->nd', gx * act, prm['peer_v'][ex])

    out = lax.map(expert_block, (hb, eb, gb)).reshape(-1, D)[:n]
    return out.reshape(B, T, D)


def layer_forward(x, c, prm, shift_prev, wkv0, pool_prev, attend):
    sh1, sc1, g1, sh2, sc2, g2 = jnp.split(jax.nn.silu(c) @ prm['w_ada'] + prm['b_ada'], 6, axis=-1)
    h = modulate(rms_norm(x, prm['norm1_g']), sh1, sc1)
    mix, new_state = mixer(h, prm, shift_prev, wkv0, pool_prev, attend)
    x = x + g1[:, None, :] * mix
    h2 = modulate(rms_norm(x, prm['norm2_g']), sh2, sc2)
    x = x + g2[:, None, :] * peer_ffn(h2, prm)
    return x, new_state


def setup_inputs(seed: int = 0) -> dict:
    key = jax.random.key(seed)
    ks = iter(jax.random.split(key, 48))
    nrm = lambda shape, scale: jax.random.normal(next(ks), shape, jnp.float32) * scale
    n_pages = PAST_LEN // PAGE_SIZE
    n_pool = (DEC_BATCH * n_pages * 5) // 4
    L, D = DEPTH, D_MODEL
    return {
        'x_prompt': nrm((BATCH, SEQ, D), 1.0),
        'x_sample': nrm((DEC_BATCH, DEC_SEQ, D), 1.0),
        'cache_k': nrm((L, n_pool, PAGE_SIZE, H_C, DH_C), 1.0),
        'cache_v': nrm((L, n_pool, PAGE_SIZE, H_C, DH_C), 1.0),
        'state_wkv': nrm((L, DEC_BATCH, H_A, DH_A, DH_A), 0.5),
        'state_shift': nrm((L, DEC_BATCH, A_PROJ), 1.0),
        'state_pool': nrm((L, DEC_BATCH, POOL_BUF, D_P), 1.0),
        'page_table': jax.random.permutation(next(ks), n_pool)[:DEC_BATCH * n_pages].reshape(DEC_BATCH, n_pages).astype(jnp.int32),
        'c_prompt': nrm((BATCH, D), 1.0),
        'c_sample': nrm((DEC_BATCH, D), 1.0),
        'w_ada': nrm((L, D, 6 * D), 0.5 * D ** -0.5),
        'b_ada': nrm((L, 6 * D), 0.02),
        'norm1_g': 1.0 + nrm((L, D), 0.05),
        'norm2_g': 1.0 + nrm((L, D), 0.05),
        'w_in': nrm((L, D, P_TOTAL), D ** -0.5),
        'mu_shift': jax.random.uniform(next(ks), (L, A_PROJ), jnp.float32),
        'w0': 0.5 + nrm((L, D_A), 0.5),
        'w_w2': nrm((L, R_W, D_A), 0.1),
        'a0': nrm((L, D_A), 0.1),
        'w_a2': nrm((L, R_A, D_A), R_A ** -0.5),
        'w_g2': nrm((L, R_G, D_A), R_G ** -0.5),
        'k_k': 0.85 + nrm((L, D_A), 0.05),
        'k_a': 1.0 + nrm((L, D_A), 0.05),
        'r_k': nrm((L, H_A, DH_A), 0.1),
        'gn_g': 1.0 + nrm((L, D_A), 0.05),
        'gn_b': nrm((L, D_A), 0.02),
        'w_pool': nrm((L, N_POOL_GROUPS, D_PG, D_PG), D_PG ** -0.5),
        'pool_scale': 1.0 + nrm((L, D_P), 0.1),
        'sb_bias': SB_BIAS_INIT + nrm((L, H_C), 0.3),
        'w_o': nrm((L, D, D), D ** -0.5),
        'w_pq': nrm((L, D, PEER_HEADS * PEER_QDIM), D ** -0.5),
        'sub_keys': nrm((L, PEER_HEADS, 2, N_KEYS, PEER_HALF), PEER_HALF ** -0.5),
        'peer_u': nrm((L, N_EXPERTS, D), D ** -0.5),
        'peer_v': nrm((L, N_EXPERTS, D), 0.5),
        'final_g': 1.0 + nrm((D,), 0.05),
    }


def reference(x_prompt, x_sample, cache_k, cache_v, state_wkv, state_shift, state_pool, page_table,
              c_prompt, c_sample, w_ada, b_ada, norm1_g, norm2_g, w_in, mu_shift, w0, w_w2, a0,
              w_a2, w_g2, k_k, k_a, r_k, gn_g, gn_b, w_pool, pool_scale, sb_bias, w_o, w_pq, sub_keys,
              peer_u, peer_v, final_g):
    xp, xs = x_prompt, x_sample
    bp = x_prompt.shape[0]
    db, n_pages = page_table.shape
    page = cache_k.shape[2]
    kp_l, vp_l, ks_l, vs_l = [], [], [], []
    wkvp_l, wkvs_l, shp_l, shs_l, poolp_l, pools_l = [], [], [], [], [], []
    for l in range(DEPTH):
        prm = {
            'w_ada': w_ada[l], 'b_ada': b_ada[l], 'norm1_g': norm1_g[l], 'norm2_g': norm2_g[l],
            'w_in': w_in[l], 'mu_shift': mu_shift[l], 'w0': w0[l], 'w_w2': w_w2[l], 'a0': a0[l],
            'w_a2': w_a2[l], 'w_g2': w_g2[l], 'k_k': k_k[l], 'k_a': k_a[l], 'r_k': r_k[l],
            'gn_g': gn_g[l], 'gn_b': gn_b[l], 'w_pool': w_pool[l], 'pool_scale': pool_scale[l],
            'sb_bias': sb_bias[l], 'w_o': w_o[l], 'w_pq': w_pq[l], 'sub_keys': sub_keys[l],
            'peer_u': peer_u[l], 'peer_v': peer_v[l],
        }
        xp, st_p = layer_forward(
            xp, c_prompt, prm,
            jnp.zeros((bp, A_PROJ), xp.dtype),
            jnp.zeros((bp, H_A, DH_A, DH_A), jnp.float32),
            jnp.zeros((bp, 0, D_P), xp.dtype),
            stick_prompt)
        k_past = cache_k[l][page_table].reshape(db, n_pages * page, H_C, DH_C)
        v_past = cache_v[l][page_table].reshape(db, n_pages * page, H_C, DH_C)
        attend_s = functools.partial(stick_sample, k_past=k_past, v_past=v_past)
        xs, st_s = layer_forward(xs, c_sample, prm, state_shift[l], state_wkv[l], state_pool[l], attend_s)
        kp_l.append(st_p[0]); vp_l.append(st_p[1]); wkvp_l.append(st_p[2]); shp_l.append(st_p[3]); poolp_l.append(st_p[4])
        ks_l.append(st_s[0]); vs_l.append(st_s[1]); wkvs_l.append(st_s[2]); shs_l.append(st_s[3]); pools_l.append(st_s[4])
    y_prompt = rms_norm(xp, final_g)
    y_sample = rms_norm(xs, final_g)
    return (y_prompt, y_sample,
            jnp.stack(kp_l), jnp.stack(vp_l), jnp.stack(ks_l), jnp.stack(vs_l),
            jnp.stack(wkvp_l), jnp.stack(wkvs_l), jnp.stack(shp_l), jnp.stack(shs_l),
            jnp.stack(poolp_l), jnp.stack(pools_l))
```

```python
import functools

import jax
import jax.numpy as jnp
from jax import lax
from jax.experimental import pallas as pl
from jax.experimental.pallas import tpu as pltpu

F32 = jnp.float32
BF16 = jnp.bfloat16

LANES = 128
VMEM_LIMIT = 56 * 1024 * 1024

DH_A = 64
N_PAIR = 8
D_A = 2 * DH_A * N_PAIR
R_LORA = 128
A_PROJ = 3 * D_A + 2 * R_LORA
GN_EPS = 64e-5
NORM_EPS = 1e-6
POOL_WINDOWS = (2, 4, 8, 16)
POOL_HALO = 16
D_PG = 256
DH_C = 128
H_C = 16
PAGE = 128
N_KEYS = 128
PEER_HEADS = 8
PEER_TOPK = 16
N_GATHER = PEER_HEADS * PEER_TOPK
PEER_SLOTS = 3


def _cparams(*sem):
    return pltpu.CompilerParams(dimension_semantics=sem, vmem_limit_bytes=VMEM_LIMIT)


def _split_bf16(x):
    hi = x.astype(BF16)
    lo = (x - hi.astype(F32)).astype(BF16)
    return hi, lo


def _dot(a, b):
    return jnp.dot(a, b, preferred_element_type=F32)


def _dot_nt(a, b):
    return lax.dot_general(a, b, (((1,), (1,)), ((), ())), preferred_element_type=F32)


def _dot01(x, m01):
    hi, lo = _split_bf16(x)
    return _dot(hi, m01) + _dot(lo, m01)


def _sigmoid(x):
    return 1.0 / (1.0 + jnp.exp(-x))


def _log_sigmoid(x):
    return jnp.minimum(x, 0.0) - jnp.log1p(jnp.exp(-jnp.abs(x)))


def _ada_kernel(c_ref, w_ref, b_ref, o_ref):
    c = c_ref[...]
    a = (c * _sigmoid(c)).astype(BF16)
    o_ref[0] = _dot(a, w_ref[0].astype(BF16)) + b_ref[0]


def _ada(c_all, w_ada, b_ada, tn=512):
    depth, d, n6 = w_ada.shape
    rows = c_all.shape[0]
    return pl.pallas_call(
        _ada_kernel,
        grid=(depth, n6 // tn),
        in_specs=[pl.BlockSpec((rows, d), lambda l, j: (0, 0)),
                  pl.BlockSpec((1, d, tn), lambda l, j: (l, 0, j)),
                  pl.BlockSpec((1, 1, tn), lambda l, j: (l, 0, j))],
        out_specs=pl.BlockSpec((1, rows, tn), lambda l, j: (l, 0, j)),
        out_shape=jax.ShapeDtypeStruct((depth, rows, n6), F32),
        compiler_params=_cparams("parallel", "parallel"),
    )(c_all, w_ada, b_ada.reshape(depth, 1, n6))


def _norm_kernel(modulate, n_out, *refs):
    if modulate:
        x_ref, g_ref, sc_ref, sh_ref = refs[:4]
        outs = refs[4:]
    else:
        x_ref, g_ref = refs[:2]
        outs = refs[2:]
    x = x_ref[...]
    y = x * lax.rsqrt(jnp.mean(x * x, axis=-1, keepdims=True) + NORM_EPS) * g_ref[...]
    if modulate:
        y = y * (1.0 + sc_ref[0]) + sh_ref[0]
    if n_out == 1:
        outs[0][...] = y.astype(outs[0].dtype)
    else:
        hi, lo = _split_bf16(y)
        outs[0][...] = y
        outs[1][...] = hi
        outs[2][...] = lo


def _group_spec(arr, tm, seq_rows, tn=None):
    g, r, d = arr.shape
    if r == 1:
        if tn is None:
            return pl.BlockSpec((1, 1, d), lambda i: (i * tm // seq_rows, 0, 0))
        return pl.BlockSpec((1, 1, tn), lambda i, j: (i * tm // seq_rows, 0, j))
    assert g == 1 and r == tm
    if tn is None:
        return pl.BlockSpec((1, tm, d), lambda i: (0, 0, 0))
    return pl.BlockSpec((1, tm, tn), lambda i, j: (0, 0, j))


def _norm(x, gain, sc, sh, seq_rows, tm, out_kinds):
    n, d = x.shape
    modulate = sc is not None
    in_specs = [pl.BlockSpec((tm, d), lambda i: (i, 0)), pl.BlockSpec((1, d), lambda i: (0, 0))]
    args = [x, gain.reshape(1, d)]
    if modulate:
        in_specs += [_group_spec(sc, tm, seq_rows), _group_spec(sh, tm, seq_rows)]
        args += [sc, sh]
    dts = [F32 if k == 'f32' else BF16 for k in out_kinds]
    outs = pl.pallas_call(
        functools.partial(_norm_kernel, modulate, len(out_kinds)),
        grid=(n // tm,),
        in_specs=in_specs,
        out_specs=[pl.BlockSpec((tm, d), lambda i: (i, 0)) for _ in dts],
        out_shape=[jax.ShapeDtypeStruct((n, d), dt) for dt in dts],
        compiler_params=_cparams("parallel"),
    )(*args)
    return outs


def _mm_kernel(n_a, n_b, pairs, residual, *refs):
    a_refs = refs[:n_a]
    b_refs = refs[n_a:n_a + n_b]
    rest = refs[n_a + n_b:]
    acc = None
    for ia, ib in pairs:
        d = _dot(a_refs[ia][...], b_refs[ib][...])
        acc = d if acc is None else acc + d
    if residual:
        x_ref, g_ref, o_ref = rest
        o_ref[...] = x_ref[...] + g_ref[0] * acc
    else:
        o_ref, = rest
        o_ref[...] = acc.astype(o_ref.dtype)


def _matmul(a_list, b_list, pairs, out_dtype, tm, tn, residual=None, seq_rows=None):
    m = a_list[0].shape[0]
    n = b_list[0].shape[1]
    in_specs = [pl.BlockSpec((tm, a.shape[1]), lambda i, j: (i, 0)) for a in a_list]
    in_specs += [pl.BlockSpec((b.shape[0], tn), lambda i, j: (0, j)) for b in b_list]
    args = list(a_list) + list(b_list)
    if residual is not None:
        x, gate = residual
        in_specs += [pl.BlockSpec((tm, tn), lambda i, j: (i, j)), _group_spec(gate, tm, seq_rows, tn)]
        args += [x, gate]
    return pl.pallas_call(
        functools.partial(_mm_kernel, len(a_list), len(b_list), tuple(pairs), residual is not None),
        grid=(m // tm, n // tn),
        in_specs=in_specs,
        out_specs=pl.BlockSpec((tm, tn), lambda i, j: (i, j)),
        out_shape=jax.ShapeDtypeStruct((m, n), out_dtype),
        compiler_params=_cparams("parallel", "parallel"),
    )(*args)


def _rwkv_prep_kernel(tc, p_ref, prev_ref, mu_ref, w0_ref, a0_ref, kk_ref, ka_ref, rk_ref,
                      ww_ref, wa_ref, wg_ref, ones_ref,
                      r_out, w_out, k_out, v_out, kk_out, b_out, g_out, bonus_out, ext_ref):
    t_blk = pl.program_id(1)

    @pl.when(t_blk == 0)
    def _():
        ext_ref[pl.ds(7, 1), :] = prev_ref[0]

    @pl.when(t_blk > 0)
    def _():
        ext_ref[pl.ds(7, 1), :] = ext_ref[pl.ds(7 + tc, 1), :]

    p = p_ref[0]
    ext_ref[pl.ds(8, tc), :] = p
    prev = ext_ref[pl.ds(7, tc), :]
    xs = p + (prev - p) * mu_ref[...]
    xr = xs[:, 0:D_A]
    xk = xs[:, D_A:2 * D_A]
    xv = xs[:, 2 * D_A:3 * D_A]
    x_wa = xs[:, 3 * D_A:3 * D_A + R_LORA]
    x_g = xs[:, 3 * D_A + R_LORA:]
    lw = _dot(jnp.tanh(x_wa).astype(BF16), ww_ref[...])
    la = _dot(x_wa.astype(BF16), wa_ref[...])
    g = _dot(_sigmoid(x_g).astype(BF16), wg_ref[...])
    w_log = -(jnp.maximum(-(w0_ref[...] + lw), 0.0) + jnp.log1p(jnp.exp(-jnp.abs(w0_ref[...] + lw)))) - 0.5
    decay = jnp.exp(-jnp.exp(w_log))
    a = _sigmoid(a0_ref[...] + la)
    kk = xk * kk_ref[...]
    k2 = xk * (1.0 + (a - 1.0) * ka_ref[...])
    rk = xr * k2 * rk_ref[...]
    ones = ones_ref[...]
    for j in range(D_A // LANES):
        sl = slice(j * LANES, (j + 1) * LANES)
        kkj = kk[:, sl]
        n2 = _dot01(kkj * kkj, ones)
        kkn = kkj / jnp.maximum(jnp.sqrt(n2), 1e-12)
        kk_out[0, :, sl] = kkn
        b_out[0, :, sl] = kkn * a[:, sl]
        bonus_out[0, :, sl] = _dot01(rk[:, sl], ones) * xv[:, sl]
    r_out[0] = xr
    w_out[0] = decay
    k_out[0] = k2
    v_out[0] = xv
    g_out[0] = g


def _rwkv_prep(p_a, shift_prev, prm, tc):
    b, t, width = p_a.shape
    row = lambda v: v.reshape(1, -1)
    zeros = jnp.zeros((DH_A, D_A), F32)
    ww = jnp.concatenate([prm['w_w2'], zeros], axis=0).astype(BF16)
    wa = jnp.concatenate([zeros, prm['w_a2']], axis=0).astype(BF16)
    lane = jnp.arange(LANES)
    ones = (lane[:, None] // DH_A == lane[None, :] // DH_A).astype(BF16)
    vec = lambda: pl.BlockSpec((1, D_A), lambda i, j: (0, 0))
    seq = lambda: pl.BlockSpec((1, tc, D_A), lambda i, j: (i, j, 0))
    outs = pl.pallas_call(
        functools.partial(_rwkv_prep_kernel, tc),
        grid=(b, t // tc),
        in_specs=[pl.BlockSpec((1, tc, width), lambda i, j: (i, j, 0)),
                  pl.BlockSpec((1, 1, width), lambda i, j: (i, 0, 0)),
                  pl.BlockSpec((1, width), lambda i, j: (0, 0)),
                  vec(), vec(), vec(), vec(), vec(),
                  pl.BlockSpec((R_LORA, D_A), lambda i, j: (0, 0)),
                  pl.BlockSpec((R_LORA, D_A), lambda i, j: (0, 0)),
                  pl.BlockSpec((R_LORA, D_A), lambda i, j: (0, 0)),
                  pl.BlockSpec((LANES, LANES), lambda i, j: (0, 0))],
        out_specs=[seq() for _ in range(8)],
        out_shape=[jax.ShapeDtypeStruct((b, t, D_A), F32) for _ in range(8)],
        scratch_shapes=[pltpu.VMEM((tc + 16, width), F32)],
        compiler_params=_cparams("parallel", "arbitrary"),
    )(p_a, shift_prev.reshape(b, 1, width), row(prm['mu_shift']), row(prm['w0']), row(prm['a0']),
      row(prm['k_k']), row(prm['k_a']), row(prm['r_k']), ww, wa, prm['w_g2'].astype(BF16), ones)
    return outs


def _rwkv_scan_kernel(tc, r_ref, w_ref, k_ref, v_ref, kk_ref, b_ref, s0_ref, y_ref, s_out_ref, s_ref):
    t_blk = pl.program_id(1)

    @pl.when(t_blk == 0)
    def _():
        s_ref[...] = s0_ref[0]

    lane = lax.broadcasted_iota(jnp.int32, (DH_A, LANES), 1)
    row = lax.broadcasted_iota(jnp.int32, (DH_A, LANES), 0)
    left = lane < DH_A
    diag = (lane % DH_A) == row

    def seg_sum(x):
        s_l = jnp.sum(jnp.where(left, x, 0.0), axis=1, keepdims=True)
        s_r = jnp.sum(jnp.where(left, 0.0, x), axis=1, keepdims=True)
        return jnp.where(left, s_l, s_r)

    rows = min(tc, 8)

    def group(gi, carry):
        base = pl.multiple_of(gi * rows, rows)
        for p in range(N_PAIR):
            sl = slice(p * LANES, (p + 1) * LANES)
            r8, w8, k8, v8, kk8, b8 = (ref[0, pl.ds(base, rows), sl]
                                       for ref in (r_ref, w_ref, k_ref, v_ref, kk_ref, b_ref))
            s = s_ref[p]
            ys = []
            for i in range(rows):
                at = lambda x: x[i:i + 1, :]
                sa = seg_sum(s * at(kk8))
                v_col = seg_sum(jnp.where(diag, at(v8), 0.0))
                s = s * at(w8) - sa * at(b8) + v_col * at(k8)
                y_b = seg_sum(s * at(r8))
                ys.append(jnp.sum(jnp.where(diag, y_b, 0.0), axis=0, keepdims=True))
            s_ref[p] = s
            y_ref[0, pl.ds(base, rows), sl] = jnp.concatenate(ys, axis=0)
        return carry

    lax.fori_loop(0, tc // rows, group, 0)

    @pl.when(t_blk == pl.num_programs(1) - 1)
    def _():
        s_out_ref[0] = s_ref[...]


def _rwkv_scan(r, w, k, v, kk, beta, s0, tc):
    b, t, _ = r.shape
    s0p = s0.astype(F32).reshape(b, N_PAIR, 2, DH_A, DH_A).transpose(0, 1, 3, 2, 4).reshape(b, N_PAIR, DH_A, LANES)
    seq = lambda: pl.BlockSpec((1, tc, D_A), lambda i, j: (i, j, 0))
    st = lambda: pl.BlockSpec((1, N_PAIR, DH_A, LANES), lambda i, j: (i, 0, 0, 0))
    y, s_fin = pl.pallas_call(
        functools.partial(_rwkv_scan_kernel, tc),
        grid=(b, t // tc),
        in_specs=[seq() for _ in range(6)] + [st()],
        out_specs=[seq(), st()],
        out_shape=[jax.ShapeDtypeStruct((b, t, D_A), F32),
                   jax.ShapeDtypeStruct((b, N_PAIR, DH_A, LANES), F32)],
        scratch_shapes=[pltpu.VMEM((N_PAIR, DH_A, LANES), F32)],
        compiler_params=_cparams("parallel", "arbitrary"),
    )(r, w, k, v, kk, beta, s0p)
    s_fin = s_fin.reshape(b, N_PAIR, DH_A, 2, DH_A).transpose(0, 1, 3, 2, 4).reshape(b, 2 * N_PAIR, DH_A, DH_A)
    return y, s_fin


def _rwkv_post_kernel(y_ref, bonus_ref, g_ref, gng_ref, gnb_ref, ones_ref, o_ref):
    ones = ones_ref[...]
    for j in range(D_A // LANES):
        sl = slice(j * LANES, (j + 1) * LANES)
        y = y_ref[:, sl]
        mean = _dot01(y, ones) * (1.0 / DH_A)
        d = y - mean
        var = _dot01(d * d, ones) * (1.0 / DH_A)
        yn = d * lax.rsqrt(var + GN_EPS) * gng_ref[:, sl] + gnb_ref[:, sl]
        o_ref[:, sl] = ((yn + bonus_ref[:, sl]) * g_ref[:, sl]).astype(o_ref.dtype)


def _rwkv_post(y, bonus, g, prm, tm):
    n = y.shape[0]
    lane = jnp.arange(LANES)
    ones = (lane[:, None] // DH_A == lane[None, :] // DH_A).astype(BF16)
    blk = lambda: pl.BlockSpec((tm, D_A), lambda i: (i, 0))
    vec = lambda: pl.BlockSpec((1, D_A), lambda i: (0, 0))
    return pl.pallas_call(
        _rwkv_post_kernel,
        grid=(n // tm,),
        in_specs=[blk(), blk(), blk(), vec(), vec(), pl.BlockSpec((LANES, LANES), lambda i: (0, 0))],
        out_specs=blk(),
        out_shape=jax.ShapeDtypeStruct((n, D_A), BF16),
        compiler_params=_cparams("parallel"),
    )(y, bonus, g, prm['gn_g'].reshape(1, D_A), prm['gn_b'].reshape(1, D_A), ones)


def _pool_kernel(tc, n_prev, u_ref, prev_ref, w_ref, scale_ref, o_ref, ext_ref):
    t_blk = pl.program_id(1)

    @pl.when(t_blk == 0)
    def _():
        ext_ref[pl.ds(0, POOL_HALO), :] = prev_ref[0]

    @pl.when(t_blk > 0)
    def _():
        ext_ref[pl.ds(0, POOL_HALO), :] = ext_ref[pl.ds(tc, POOL_HALO), :]

    u = u_ref[0]
    ext_ref[pl.ds(POOL_HALO, tc), :] = u
    pos = n_prev + t_blk * tc + lax.broadcasted_iota(jnp.int32, (tc, 1), 0) + 1
    for gi, win in enumerate(POOL_WINDOWS):
        sl = slice(gi * D_PG, (gi + 1) * D_PG)
        tot = u[:, sl]
        for j in range(1, win):
            tot = tot + ext_ref[pl.ds(POOL_HALO - j, tc), sl]
        cnt = jnp.minimum(pos, win).astype(F32)
        diff = tot / cnt - u[:, sl]
        o_ref[0, :, sl] = (_dot(diff.astype(BF16), w_ref[gi]) * scale_ref[:, sl]).astype(o_ref.dtype)


def _pool(u, u_prev, prm, tc):
    b, t, d_p = u.shape
    n_prev = u_prev.shape[1]
    halo = jnp.concatenate([jnp.zeros((b, POOL_HALO - n_prev, d_p), F32), u_prev.astype(F32)], axis=1)
    return pl.pallas_call(
        functools.partial(_pool_kernel, tc, n_prev),
        grid=(b, t // tc),
        in_specs=[pl.BlockSpec((1, tc, d_p), lambda i, j: (i, j, 0)),
                  pl.BlockSpec((1, POOL_HALO, d_p), lambda i, j: (i, 0, 0)),
                  pl.BlockSpec((len(POOL_WINDOWS), D_PG, D_PG), lambda i, j: (0, 0, 0)),
                  pl.BlockSpec((1, d_p), lambda i, j: (0, 0))],
        out_specs=pl.BlockSpec((1, tc, d_p), lambda i, j: (i, j, 0)),
        out_shape=jax.ShapeDtypeStruct((b, t, d_p), BF16),
        scratch_shapes=[pltpu.VMEM((tc + POOL_HALO, d_p), F32)],
        compiler_params=_cparams("parallel", "arbitrary"),
    )(u, halo, prm['w_pool'].astype(BF16), prm['pool_scale'].reshape(1, d_p))


def _stick_block(q, kb, vb, bias, mask, carry, upper):
    z = _dot_nt(q, kb) * (DH_C ** -0.5) + bias
    ls = _log_sigmoid(z)
    l1m = ls - z
    if mask is not None:
        l1m = jnp.where(mask, l1m, 0.0)
    suffix = _dot01(l1m, upper)
    wts = jnp.exp(ls + suffix + carry)
    if mask is not None:
        wts = jnp.where(mask, wts, 0.0)
    return _dot(wts.astype(BF16), vb), carry + jnp.sum(l1m, axis=1, keepdims=True)


def _stick_prompt_kernel(q_ref, k_ref, v_ref, bias_ref, upper_ref, o_ref):
    h = pl.program_id(1)
    i = pl.program_id(2)
    q = q_ref[0]
    bias = bias_ref[h]
    upper = upper_ref[...]
    row = lax.broadcasted_iota(jnp.int32, (PAGE, PAGE), 0)
    col = lax.broadcasted_iota(jnp.int32, (PAGE, PAGE), 1)

    def body(jj, state):
        carry, acc = state
        start = pl.multiple_of((i - jj) * PAGE, PAGE)
        kb = k_ref[0, pl.ds(start, PAGE), :].astype(BF16)
        vb = v_ref[0, pl.ds(start, PAGE), :].astype(BF16)
        mask = jnp.logical_or(jj > 0, col < row)
        out, carry = _stick_block(q, kb, vb, bias, mask, carry, upper)
        return carry, acc + out

    _, acc = lax.fori_loop(0, i + 1, body, (jnp.zeros((PAGE, 1), F32), jnp.zeros((PAGE, DH_C), F32)))
    o_ref[0] = acc.astype(o_ref.dtype)


def _upper01():
    i = jnp.arange(PAGE)
    return (i[:, None] > i[None, :]).astype(BF16)


def _stick_prompt(q, k, v, bias):
    b, t, _ = q.shape
    return pl.pallas_call(
        _stick_prompt_kernel,
        grid=(b, H_C, t // PAGE),
        in_specs=[pl.BlockSpec((1, PAGE, DH_C), lambda bi, h, i: (bi, i, h)),
                  pl.BlockSpec((1, t, DH_C), lambda bi, h, i: (bi, 0, h)),
                  pl.BlockSpec((1, t, DH_C), lambda bi, h, i: (bi, 0, h)),
                  pl.BlockSpec(memory_space=pltpu.SMEM),
                  pl.BlockSpec((PAGE, PAGE), lambda bi, h, i: (0, 0))],
        out_specs=pl.BlockSpec((1, PAGE, DH_C), lambda bi, h, i: (bi, i, h)),
        out_shape=jax.ShapeDtypeStruct((b, t, H_C * DH_C), BF16),
        compiler_params=_cparams("parallel", "parallel", "arbitrary"),
    )(q, k, v, bias.astype(F32), _upper01())


def _stick_sample_kernel(n_q, pt_ref, q_ref, kn_ref, vn_ref, kc_ref, vc_ref, bias_ref, upper_ref, o_ref,
                         carry_ref, acc_ref):
    j = pl.program_id(1)
    q = q_ref[0]
    bias = bias_ref[...]
    upper = upper_ref[...]

    @pl.when(j == 0)
    def _():
        rows = H_C * n_q
        q_idx = lax.broadcasted_iota(jnp.int32, (rows, PAGE), 0) % n_q
        col = lax.broadcasted_iota(jnp.int32, (rows, PAGE), 1)
        out, carry = _stick_block(q, kn_ref[0].astype(BF16), vn_ref[0].astype(BF16), bias, col < q_idx,
                                  jnp.zeros((rows, 1), F32), upper)
        acc_ref[...] = out
        carry_ref[...] = carry

    @pl.when(j > 0)
    def _():
        out, carry = _stick_block(q, kc_ref[0, 0].astype(BF16), vc_ref[0, 0].astype(BF16), bias, None,
                                  carry_ref[...], upper)
        acc_ref[...] += out
        carry_ref[...] = carry

    @pl.when(j == pl.num_programs(1) - 1)
    def _():
        for h in range(H_C):
            sl = slice(h * DH_C, (h + 1) * DH_C)
            o_ref[0, :, sl] = acc_ref[h * n_q:(h + 1) * n_q, sl].astype(o_ref.dtype)


def _stick_sample(q, k_new, v_new, cache_k, cache_v, layer, page_table, bias):
    b, n_q, width = k_new.shape
    n_pages = page_table.shape[1]
    depth, n_pool = cache_k.shape[:2]
    rows = H_C * n_q
    qh = q.reshape(b, n_q, H_C, DH_C).transpose(0, 2, 1, 3)
    eye = jnp.eye(H_C, dtype=q.dtype)
    q_bd = (qh[:, :, :, None, :] * eye[None, :, None, :, None]).reshape(b, rows, width).astype(BF16)
    pad = lambda x: jnp.pad(x, ((0, 0), (0, PAGE - n_q), (0, 0)))
    bias_rows = jnp.repeat(bias.astype(F32), n_q).reshape(rows, 1)
    page = lambda bi, j, pt: (layer, pt[bi, n_pages - jnp.maximum(j, 1)], 0, 0)
    grid_spec = pltpu.PrefetchScalarGridSpec(
        num_scalar_prefetch=1,
        grid=(b, n_pages + 1),
        in_specs=[pl.BlockSpec((1, rows, width), lambda bi, j, pt: (bi, 0, 0)),
                  pl.BlockSpec((1, PAGE, width), lambda bi, j, pt: (bi, 0, 0)),
                  pl.BlockSpec((1, PAGE, width), lambda bi, j, pt: (bi, 0, 0)),
                  pl.BlockSpec((1, 1, PAGE, width), page),
                  pl.BlockSpec((1, 1, PAGE, width), page),
                  pl.BlockSpec((rows, 1), lambda bi, j, pt: (0, 0)),
                  pl.BlockSpec((PAGE, PAGE), lambda bi, j, pt: (0, 0))],
        out_specs=pl.BlockSpec((1, n_q, width), lambda bi, j, pt: (bi, 0, 0)),
        scratch_shapes=[pltpu.VMEM((rows, 1), F32), pltpu.VMEM((rows, width), F32)],
    )
    return pl.pallas_call(
        functools.partial(_stick_sample_kernel, n_q),
        grid_spec=grid_spec,
        out_shape=jax.ShapeDtypeStruct((b, n_q, width), BF16),
        compiler_params=_cparams("parallel", "arbitrary"),
    )(page_table, q_bd, pad(k_new), pad(v_new),
      cache_k.reshape(depth, n_pool, PAGE, width), cache_v.reshape(depth, n_pool, PAGE, width),
      bias_rows, _upper01())


def _top_rounds(x, lane_f, n_rounds, payload=None):
    big = float(x.shape[1])
    vals, tags = [], []
    for _ in range(n_rounds):
        m = jnp.max(x, axis=1, keepdims=True)
        pos = jnp.min(jnp.where(x == m, lane_f, big), axis=1, keepdims=True)
        hit = lane_f == pos
        vals.append(m)
        tags.append(pos if payload is None else jnp.max(jnp.where(hit, payload, -1.0), axis=1, keepdims=True))
        x = jnp.where(hit, -jnp.inf, x)
    return vals, tags


def _peer_route_kernel(q_ref, keys_ref, idx_ref, gate_ref):
    tb = q_ref.shape[0]
    n_cand = PEER_TOPK * PEER_TOPK
    lane_k = lax.broadcasted_iota(jnp.int32, (tb, N_KEYS), 1).astype(F32)
    lane_c = lax.broadcasted_iota(jnp.int32, (tb, n_cand), 1)
    lane_cf = lane_c.astype(F32)
    c_hi = lane_c // PEER_TOPK
    c_lo = lane_c % PEER_TOPK
    lane_o = lax.broadcasted_iota(jnp.int32, (tb, N_GATHER), 1)
    idx_acc = jnp.zeros((tb, N_GATHER), F32)
    gate_acc = jnp.zeros((tb, N_GATHER), F32)
    for h in range(PEER_HEADS):
        tops = []
        for p in range(2):
            c0 = (2 * h + p) * N_KEYS
            q_hi, q_lo = _split_bf16(q_ref[:, c0:c0 + N_KEYS])
            k_hi, k_lo = _split_bf16(keys_ref[2 * h + p])
            s = _dot_nt(q_hi, k_hi) + (_dot_nt(q_lo, k_hi) + _dot_nt(q_hi, k_lo))
            tops.append(_top_rounds(s, lane_k, PEER_TOPK))
        (v1, i1), (v2, i2) = tops
        cand = jnp.zeros((tb, n_cand), F32)
        cidx = jnp.zeros((tb, n_cand), F32)
        for r in range(PEER_TOPK):
            cand = cand + jnp.where(c_hi == r, v1[r], 0.0) + jnp.where(c_lo == r, v2[r], 0.0)
            cidx = cidx + jnp.where(c_hi == r, i1[r] * float(N_KEYS), 0.0) + jnp.where(c_lo == r, i2[r], 0.0)
        top, eidx = _top_rounds(cand, lane_cf, PEER_TOPK, payload=cidx)
        ex = [jnp.exp(t - top[0]) for t in top]
        denom = functools.reduce(lambda a, b: a + b, ex)
        for r in range(PEER_TOPK):
            hit = lane_o == h * PEER_TOPK + r
            idx_acc = jnp.where(hit, eidx[r], idx_acc)
            gate_acc = jnp.where(hit, ex[r] / denom, gate_acc)
    idx_ref[...] = idx_acc.astype(jnp.int32)
    gate_ref[...] = gate_acc


def _peer_route(q, sub_keys, tb):
    n, width = q.shape
    keys = sub_keys.reshape(2 * PEER_HEADS, N_KEYS, N_KEYS)
    return pl.pallas_call(
        _peer_route_kernel,
        grid=(n // tb,),
        in_specs=[pl.BlockSpec((tb, width), lambda i: (i, 0)),
                  pl.BlockSpec(keys.shape, lambda i: (0, 0, 0))],
        out_specs=[pl.BlockSpec((tb, N_GATHER), lambda i: (i, 0)) for _ in range(2)],
        out_shape=[jax.ShapeDtypeStruct((n, N_GATHER), jnp.int32), jax.ShapeDtypeStruct((n, N_GATHER), F32)],
        compiler_params=_cparams("parallel"),
    )(q, keys)


def _gelu_tanh(x):
    return 0.5 * x * (1.0 + jnp.tanh(0.7978845608028654 * (x + 0.044715 * x * x * x)))


def _peer_expert_kernel(tb, per_row_gate, idx_ref, h_ref, gate_ref, x_ref, g2_ref, u_hbm, v_hbm, o_ref,
                        ubuf, vbuf, sems):
    def row_copies(t, j, slot):
        e = idx_ref[t, j]
        return (pltpu.make_async_copy(u_hbm.at[pl.ds(e, 1), :], ubuf.at[slot, pl.ds(j, 1), :], sems.at[0, slot]),
                pltpu.make_async_copy(v_hbm.at[pl.ds(e, 1), :], vbuf.at[slot, pl.ds(j, 1), :], sems.at[1, slot]))

    def issue(t, slot):
        def body(j, c):
            cu, cv = row_copies(t, j, slot)
            cu.start()
            cv.start()
            return c
        lax.fori_loop(0, N_GATHER, body, 0)

    def wait(slot):
        pltpu.make_async_copy(u_hbm.at[pl.ds(0, N_GATHER), :], ubuf.at[slot], sems.at[0, slot]).wait()
        pltpu.make_async_copy(v_hbm.at[pl.ds(0, N_GATHER), :], vbuf.at[slot], sems.at[1, slot]).wait()

    ahead = PEER_SLOTS - 1
    for t0 in range(ahead):
        issue(t0, t0)

    def group(gi, c):
        base = pl.multiple_of(gi * 8, 8)
        h8 = h_ref[pl.ds(base, 8), :]
        gate8 = gate_ref[pl.ds(base, 8), :]
        mixes = []
        for i in range(8):
            t = base + i
            slot = t % PEER_SLOTS

            @pl.when(t + ahead < tb)
            def _():
                issue(t + ahead, (t + ahead) % PEER_SLOTS)

            wait(slot)
            hb = jnp.broadcast_to(h8[i:i + 1, :], h8.shape).astype(BF16)
            act = _gelu_tanh(_dot_nt(hb, ubuf[slot].astype(BF16)))
            coef = (gate8[i:i + 1, :] * act).astype(BF16)
            mixes.append(_dot(coef, vbuf[slot].astype(BF16))[0:1])
        g2 = g2_ref[0, pl.ds(base, 8), :] if per_row_gate else g2_ref[0]
        o_ref[pl.ds(base, 8), :] = x_ref[pl.ds(base, 8), :] + g2 * jnp.concatenate(mixes, axis=0)
        return c

    lax.fori_loop(0, tb // 8, group, 0)


def _peer_expert(h2, idx, gate, x, g2, peer_u, peer_v, seq_rows, tb):
    n, d = h2.shape
    per_row_gate = g2.shape[1] != 1
    blk = lambda w: pl.BlockSpec((tb, w), lambda i: (i, 0))
    return pl.pallas_call(
        functools.partial(_peer_expert_kernel, tb, per_row_gate),
        grid=(n // tb,),
        in_specs=[pl.BlockSpec((tb, N_GATHER), lambda i: (i, 0), memory_space=pltpu.SMEM),
                  blk(d), blk(N_GATHER), blk(d), _group_spec(g2, tb, seq_rows),
                  pl.BlockSpec(memory_space=pl.ANY), pl.BlockSpec(memory_space=pl.ANY)],
        out_specs=blk(d),
        out_shape=jax.ShapeDtypeStruct((n, d), F32),
        scratch_shapes=[pltpu.VMEM((PEER_SLOTS, N_GATHER, d), F32), pltpu.VMEM((PEER_SLOTS, N_GATHER, d), F32),
                        pltpu.SemaphoreType.DMA((2, PEER_SLOTS))],
        compiler_params=_cparams("arbitrary"),
    )(idx, h2, gate, x, g2, peer_u, peer_v)


def _tile(n, pref):
    return pref if n % pref == 0 else n


def _layer(x, mods, prm, wts, shift_prev, wkv0, pool_prev, attend, seq_len):
    n, d = x.shape
    b = n // seq_len
    sh1, sc1, g1, sh2, sc2, g2 = mods
    tm = _tile(n, 512)
    h, = _norm(x, prm['norm1_g'], sc1, sh1, seq_len, _tile(n, 256), ('bf16',))
    proj = lambda w, dt, tn: _matmul([h], [w], [(0, 0)], dt, tm, tn)
    p_a = proj(wts['in_a'], F32, 256).reshape(b, seq_len, A_PROJ)
    u = proj(wts['in_u'], F32, 512).reshape(b, seq_len, -1)
    q_c = proj(wts['in_q'], BF16 if attend == 'prompt' else F32, 512).reshape(b, seq_len, -1)
    k_c = proj(wts['in_k'], F32, 512).reshape(b, seq_len, -1)
    v_c = proj(wts['in_v'], F32, 512).reshape(b, seq_len, -1)

    tc = _tile(seq_len, 256)
    r, w, k, v, kk, beta, g, bonus = _rwkv_prep(p_a, shift_prev, prm, tc)
    y, wkv_new = _rwkv_scan(r, w, k, v, kk, beta, wkv0, tc)
    flat = lambda a: a.reshape(n, -1)
    o_a = _rwkv_post(flat(y), flat(bonus), flat(g), prm, _tile(n, 512))
    o_b = flat(_pool(u, pool_prev, prm, tc))
    if attend == 'prompt':
        o_c = flat(_stick_prompt(q_c, k_c, v_c, prm['sb_bias']))
    else:
        o_c = flat(_stick_sample(q_c, k_c, v_c, *attend, prm['sb_bias']))

    d_a, d_p = o_a.shape[1], o_b.shape[1]
    w_o = wts['o']
    x = _matmul([o_a, o_b, o_c], [w_o[:d_a], w_o[d_a:d_a + d_p], w_o[d_a + d_p:]], [(0, 0), (1, 1), (2, 2)],
                F32, tm, 512, residual=(x, g1), seq_rows=seq_len)

    h2, h2_hi, h2_lo = _norm(x, prm['norm2_g'], sc2, sh2, seq_len, _tile(n, 256), ('f32', 'bf16', 'bf16'))
    q_p = _matmul([h2_hi, h2_lo], [wts['pq_hi'], wts['pq_lo']], [(0, 0), (1, 0), (0, 1)], F32, tm, 512)
    eidx, gate = _peer_route(q_p, prm['sub_keys'], _tile(n, 256))
    x = _peer_expert(h2, eidx, gate, x, g2, prm['peer_u'], prm['peer_v'], seq_len, _tile(n, 64))

    pool_ext = jnp.concatenate([pool_prev.astype(u.dtype), u], axis=1)
    state = (k_c.reshape(b, seq_len, H_C, DH_C), v_c.reshape(b, seq_len, H_C, DH_C), wkv_new,
             p_a[:, -1], pool_ext[:, pool_ext.shape[1] - (POOL_HALO - 1):])
    return x, state


def kernel(x_prompt, x_sample, cache_k, cache_v, state_wkv, state_shift, state_pool, page_table, c_prompt, c_sample, w_ada, b_ada, norm1_g, norm2_g, w_in, mu_shift, w0, w_w2, a0, w_a2, w_g2, k_k, k_a, r_k, gn_g, gn_b, w_pool, pool_scale, sb_bias, w_o, w_pq, sub_keys, peer_u, peer_v, final_g):
    depth = w_ada.shape[0]
    bp, seq, d = x_prompt.shape
    db, dseq, _ = x_sample.shape
    d_p = state_pool.shape[-1]
    d_c = H_C * DH_C

    c_all = jnp.concatenate([c_prompt, c_sample, jnp.zeros((-(bp + db) % 8, d), F32)], axis=0)
    mod_all = _ada(c_all, w_ada, b_ada).reshape(depth, c_all.shape[0], 6, d)

    xp = x_prompt.reshape(bp * seq, d)
    xs = x_sample.reshape(db * dseq, d)
    outs_p = [[] for _ in range(5)]
    outs_s = [[] for _ in range(5)]
    for l in range(depth):
        prm = {'norm1_g': norm1_g[l], 'norm2_g': norm2_g[l], 'mu_shift': mu_shift[l], 'w0': w0[l], 'w_w2': w_w2[l],
               'a0': a0[l], 'w_a2': w_a2[l], 'w_g2': w_g2[l], 'k_k': k_k[l], 'k_a': k_a[l], 'r_k': r_k[l].reshape(-1),
               'gn_g': gn_g[l], 'gn_b': gn_b[l], 'w_pool': w_pool[l], 'pool_scale': pool_scale[l],
               'sb_bias': sb_bias[l], 'sub_keys': sub_keys[l], 'peer_u': peer_u[l], 'peer_v': peer_v[l]}
        w_in_l = w_in[l]
        o_u, o_q = A_PROJ, A_PROJ + d_p
        pq_hi = w_pq[l].astype(BF16)
        wts = {'in_a': w_in_l[:, :o_u].astype(BF16), 'in_u': w_in_l[:, o_u:o_q].astype(BF16),
               'in_q': w_in_l[:, o_q:o_q + d_c].astype(BF16), 'in_k': w_in_l[:, o_q + d_c:o_q + 2 * d_c].astype(BF16),
               'in_v': w_in_l[:, o_q + 2 * d_c:].astype(BF16), 'o': w_o[l].astype(BF16),
               'pq_hi': pq_hi, 'pq_lo': (w_pq[l] - pq_hi.astype(F32)).astype(BF16)}
        mods_p = [mod_all[l, :bp, i].reshape(bp, 1, d) for i in range(6)]
        mods_s = [jnp.repeat(mod_all[l, bp:bp + db, i], dseq, axis=0).reshape(1, db * dseq, d) for i in range(6)]
        xp, st_p = _layer(xp, mods_p, prm, wts, jnp.zeros((bp, A_PROJ), F32),
                          jnp.zeros((bp, 2 * N_PAIR, DH_A, DH_A), F32), jnp.zeros((bp, 0, d_p), F32), 'prompt', seq)
        xs, st_s = _layer(xs, mods_s, prm, wts, state_shift[l], state_wkv[l], state_pool[l],
                          (cache_k, cache_v, l, page_table), dseq)
        for i in range(5):
            outs_p[i].append(st_p[i])
            outs_s[i].append(st_s[i])
    y_prompt, = _norm(xp, final_g, None, None, seq, _tile(bp * seq, 256), ('f32',))
    y_sample, = _norm(xs, final_g, None, None, dseq, _tile(db * dseq, 256), ('f32',))
    stk = jnp.stack
    (kp, vp, wkvp, shp, poolp), (ks, vs, wkvs, shs, pools) = outs_p, outs_s
    return (y_prompt.reshape(bp, seq, d), y_sample.reshape(db, dseq, d), stk(kp), stk(vp), stk(ks), stk(vs),
            stk(wkvp), stk(wkvs), stk(shp), stk(shs), stk(poolp), stk(pools))
```

```python
import functools

import jax
import jax.numpy as jnp
from jax import lax
from jax.experimental import pallas as pl
from jax.experimental.pallas import tpu as pltpu

F32 = jnp.float32
BF16 = jnp.bfloat16

LANES = 128
VMEM_LIMIT = 56 * 1024 * 1024

DH_A = 64
N_PAIR = 8
D_A = 2 * DH_A * N_PAIR
R_LORA = 128
A_PROJ = 3 * D_A + 2 * R_LORA
GN_EPS = 64e-5
NORM_EPS = 1e-6
POOL_WINDOWS = (2, 4, 8, 16)
POOL_HALO = 16
D_PG = 256
DH_C = 128
H_C = 16
PAGE = 128
N_KEYS = 128
PEER_HEADS = 8
PEER_TOPK = 16
N_GATHER = PEER_HEADS * PEER_TOPK
PEER_GROUP = 8


def _cparams(*sem):
    return pltpu.CompilerParams(dimension_semantics=sem, vmem_limit_bytes=VMEM_LIMIT)


def _split_bf16(x):
    hi = x.astype(BF16)
    lo = (x - hi.astype(F32)).astype(BF16)
    return hi, lo


def _dot(a, b):
    return jnp.dot(a, b, preferred_element_type=F32)


def _dot_nt(a, b):
    return lax.dot_general(a, b, (((1,), (1,)), ((), ())), preferred_element_type=F32)


def _dot01(x, m01):
    hi, lo = _split_bf16(x)
    return _dot(hi, m01) + _dot(lo, m01)


def _head_ones():
    lane = jnp.arange(LANES)
    return (lane[:, None] // DH_A == lane[None, :] // DH_A).astype(BF16)


def _sigmoid(x):
    return 1.0 / (1.0 + jnp.exp(-x))


def _log_sigmoid(x):
    return jnp.minimum(x, 0.0) - jnp.log1p(jnp.exp(-jnp.abs(x)))


def _ada_kernel(c_ref, w_ref, b_ref, o_ref):
    c = c_ref[...]
    a = (c * _sigmoid(c)).astype(BF16)
    o_ref[0] = _dot(a, w_ref[0].astype(BF16)) + b_ref[0]


def _ada(c_all, w_ada, b_ada, tn=512):
    depth, d, n6 = w_ada.shape
    rows = c_all.shape[0]
    return pl.pallas_call(
        _ada_kernel,
        grid=(depth, n6 // tn),
        in_specs=[pl.BlockSpec((rows, d), lambda l, j: (0, 0)),
                  pl.BlockSpec((1, d, tn), lambda l, j: (l, 0, j)),
                  pl.BlockSpec((1, 1, tn), lambda l, j: (l, 0, j))],
        out_specs=pl.BlockSpec((1, rows, tn), lambda l, j: (l, 0, j)),
        out_shape=jax.ShapeDtypeStruct((depth, rows, n6), F32),
        compiler_params=_cparams("parallel", "parallel"),
    )(c_all, w_ada, b_ada.reshape(depth, 1, n6))


def _norm_kernel(modulate, n_out, *refs):
    if modulate:
        x_ref, g_ref, sc_ref, sh_ref = refs[:4]
        outs = refs[4:]
    else:
        x_ref, g_ref = refs[:2]
        outs = refs[2:]
    x = x_ref[...]
    y = x * lax.rsqrt(jnp.mean(x * x, axis=-1, keepdims=True) + NORM_EPS) * g_ref[...]
    if modulate:
        y = y * (1.0 + sc_ref[0]) + sh_ref[0]
    if n_out == 1:
        outs[0][...] = y.astype(outs[0].dtype)
    else:
        hi, lo = _split_bf16(y)
        outs[0][...] = y
        outs[1][...] = hi
        outs[2][...] = lo


def _group_spec(arr, tm, seq_rows, tn=None):
    g, r, d = arr.shape
    if r == 1:
        if tn is None:
            return pl.BlockSpec((1, 1, d), lambda i: (i * tm // seq_rows, 0, 0))
        return pl.BlockSpec((1, 1, tn), lambda i, j: (i * tm // seq_rows, 0, j))
    assert g == 1 and r == tm
    if tn is None:
        return pl.BlockSpec((1, tm, d), lambda i: (0, 0, 0))
    return pl.BlockSpec((1, tm, tn), lambda i, j: (0, 0, j))


def _norm(x, gain, sc, sh, seq_rows, tm, out_kinds):
    n, d = x.shape
    modulate = sc is not None
    in_specs = [pl.BlockSpec((tm, d), lambda i: (i, 0)), pl.BlockSpec((1, d), lambda i: (0, 0))]
    args = [x, gain.reshape(1, d)]
    if modulate:
        in_specs += [_group_spec(sc, tm, seq_rows), _group_spec(sh, tm, seq_rows)]
        args += [sc, sh]
    dts = [F32 if k == 'f32' else BF16 for k in out_kinds]
    outs = pl.pallas_call(
        functools.partial(_norm_kernel, modulate, len(out_kinds)),
        grid=(n // tm,),
        in_specs=in_specs,
        out_specs=[pl.BlockSpec((tm, d), lambda i: (i, 0)) for _ in dts],
        out_shape=[jax.ShapeDtypeStruct((n, d), dt) for dt in dts],
        compiler_params=_cparams("parallel"),
    )(*args)
    return outs


def _mm_kernel(n_a, n_b, pairs, residual, *refs):
    a_refs = refs[:n_a]
    b_refs = refs[n_a:n_a + n_b]
    rest = refs[n_a + n_b:]
    acc = None
    for ia, ib in pairs:
        d = _dot(a_refs[ia][...], b_refs[ib][...])
        acc = d if acc is None else acc + d
    if residual:
        x_ref, g_ref, o_ref = rest
        o_ref[...] = x_ref[...] + g_ref[0] * acc
    else:
        o_ref, = rest
        o_ref[...] = acc.astype(o_ref.dtype)


def _matmul(a_list, b_list, pairs, out_dtype, tm, tn, residual=None, seq_rows=None):
    m = a_list[0].shape[0]
    n = b_list[0].shape[1]
    in_specs = [pl.BlockSpec((tm, a.shape[1]), lambda i, j: (i, 0)) for a in a_list]
    in_specs += [pl.BlockSpec((b.shape[0], tn), lambda i, j: (0, j)) for b in b_list]
    args = list(a_list) + list(b_list)
    if residual is not None:
        x, gate = residual
        in_specs += [pl.BlockSpec((tm, tn), lambda i, j: (i, j)), _group_spec(gate, tm, seq_rows, tn)]
        args += [x, gate]
    return pl.pallas_call(
        functools.partial(_mm_kernel, len(a_list), len(b_list), tuple(pairs), residual is not None),
        grid=(m // tm, n // tn),
        in_specs=in_specs,
        out_specs=pl.BlockSpec((tm, tn), lambda i, j: (i, j)),
        out_shape=jax.ShapeDtypeStruct((m, n), out_dtype),
        compiler_params=_cparams("parallel", "parallel"),
    )(*args)


def _rwkv_prep_kernel(tc, p_ref, prev_ref, mu_ref, w0_ref, a0_ref, kk_ref, ka_ref, rk_ref,
                      ww_ref, wa_ref, wg_ref, ones_ref,
                      r_out, w_out, k_out, v_out, kk_out, b_out, g_out, bonus_out, ext_ref):
    t_blk = pl.program_id(1)

    @pl.when(t_blk == 0)
    def _():
        ext_ref[pl.ds(7, 1), :] = prev_ref[0]

    @pl.when(t_blk > 0)
    def _():
        ext_ref[pl.ds(7, 1), :] = ext_ref[pl.ds(7 + tc, 1), :]

    p = p_ref[0]
    ext_ref[pl.ds(8, tc), :] = p
    prev = ext_ref[pl.ds(7, tc), :]
    xs = p + (prev - p) * mu_ref[...]
    xr = xs[:, 0:D_A]
    xk = xs[:, D_A:2 * D_A]
    xv = xs[:, 2 * D_A:3 * D_A]
    x_wa = xs[:, 3 * D_A:3 * D_A + R_LORA]
    x_g = xs[:, 3 * D_A + R_LORA:]
    lw = _dot(jnp.tanh(x_wa).astype(BF16), ww_ref[...])
    la = _dot(x_wa.astype(BF16), wa_ref[...])
    g = _dot(_sigmoid(x_g).astype(BF16), wg_ref[...])
    w_log = -(jnp.maximum(-(w0_ref[...] + lw), 0.0) + jnp.log1p(jnp.exp(-jnp.abs(w0_ref[...] + lw)))) - 0.5
    decay = jnp.exp(-jnp.exp(w_log))
    a = _sigmoid(a0_ref[...] + la)
    kk = xk * kk_ref[...]
    k2 = xk * (1.0 + (a - 1.0) * ka_ref[...])
    rk = xr * k2 * rk_ref[...]
    ones = ones_ref[...]
    for j in range(D_A // LANES):
        sl = slice(j * LANES, (j + 1) * LANES)
        kkj = kk[:, sl]
        n2 = _dot01(kkj * kkj, ones)
        kkn = kkj / jnp.maximum(jnp.sqrt(n2), 1e-12)
        kk_out[0, :, sl] = kkn
        b_out[0, :, sl] = kkn * a[:, sl]
        bonus_out[0, :, sl] = _dot01(rk[:, sl], ones) * xv[:, sl]
    r_out[0] = xr
    w_out[0] = decay
    k_out[0] = k2
    v_out[0] = xv
    g_out[0] = g


def _rwkv_prep(p_a, shift_prev, prm, tc):
    b, t, width = p_a.shape
    row = lambda v: v.reshape(1, -1)
    zeros = jnp.zeros((DH_A, D_A), F32)
    ww = jnp.concatenate([prm['w_w2'], zeros], axis=0).astype(BF16)
    wa = jnp.concatenate([zeros, prm['w_a2']], axis=0).astype(BF16)
    ones = _head_ones()
    vec = lambda: pl.BlockSpec((1, D_A), lambda i, j: (0, 0))
    seq = lambda: pl.BlockSpec((1, tc, D_A), lambda i, j: (i, j, 0))
    outs = pl.pallas_call(
        functools.partial(_rwkv_prep_kernel, tc),
        grid=(b, t // tc),
        in_specs=[pl.BlockSpec((1, tc, width), lambda i, j: (i, j, 0)),
                  pl.BlockSpec((1, 1, width), lambda i, j: (i, 0, 0)),
                  pl.BlockSpec((1, width), lambda i, j: (0, 0)),
                  vec(), vec(), vec(), vec(), vec(),
                  pl.BlockSpec((R_LORA, D_A), lambda i, j: (0, 0)),
                  pl.BlockSpec((R_LORA, D_A), lambda i, j: (0, 0)),
                  pl.BlockSpec((R_LORA, D_A), lambda i, j: (0, 0)),
                  pl.BlockSpec((LANES, LANES), lambda i, j: (0, 0))],
        out_specs=[seq() for _ in range(8)],
        out_shape=[jax.ShapeDtypeStruct((b, t, D_A), F32) for _ in range(8)],
        scratch_shapes=[pltpu.VMEM((tc + 16, width), F32)],
        compiler_params=_cparams("parallel", "arbitrary"),
    )(p_a, shift_prev.reshape(b, 1, width), row(prm['mu_shift']), row(prm['w0']), row(prm['a0']),
      row(prm['k_k']), row(prm['k_a']), row(prm['r_k']), ww, wa, prm['w_g2'].astype(BF16), ones)
    return outs


def _rwkv_scan_kernel(tc, r_ref, w_ref, k_ref, v_ref, kk_ref, b_ref, s0_ref, ones_ref, y_ref, s_out_ref, s_ref):
    t_blk = pl.program_id(1)

    @pl.when(t_blk == 0)
    def _():
        s_ref[...] = s0_ref[0]

    lane = lax.broadcasted_iota(jnp.int32, (DH_A, LANES), 1)
    row = lax.broadcasted_iota(jnp.int32, (DH_A, LANES), 0)
    left = lane < DH_A
    diag = (lane % DH_A) == row
    lane8 = lax.broadcasted_iota(jnp.int32, (8, LANES), 1)
    row8 = lax.broadcasted_iota(jnp.int32, (8, LANES), 0)
    head_row = row8 == lane8 // DH_A
    ones = ones_ref[...]

    def seg_sum(x):
        s_l = jnp.sum(jnp.where(left, x, 0.0), axis=1, keepdims=True)
        s_r = jnp.sum(jnp.where(left, 0.0, x), axis=1, keepdims=True)
        return jnp.where(left, s_l, s_r)

    rows = min(tc, 8)

    def group(gi, carry):
        base = pl.multiple_of(gi * rows, rows)
        ops, v_cols = [], []
        for p in range(N_PAIR):
            sl = slice(p * LANES, (p + 1) * LANES)
            ops.append([ref[0, pl.ds(base, rows), sl] for ref in (r_ref, w_ref, k_ref, v_ref, kk_ref, b_ref)])
            v8 = ops[p][3]
            v_diag = jnp.concatenate([jnp.where(diag, v8[i:i + 1, :], 0.0) for i in range(rows)], axis=0)
            v_cols.append(_dot01(v_diag, ones))
        ys = [[] for _ in range(N_PAIR)]
        for i in range(rows):
            for p in range(N_PAIR):
                r8, w8, k8, _, kk8, b8 = (x[i:i + 1, :] for x in ops[p])
                s = s_ref[p]
                sa = seg_sum(s * kk8)
                s = s * w8 - sa * b8 + v_cols[p][i * DH_A:(i + 1) * DH_A, :] * k8
                s_ref[p] = s
                s_bf = s.astype(BF16)
                r2 = jnp.where(head_row, r8, 0.0).astype(BF16)
                yy = _dot_nt(r2, jnp.concatenate([s_bf, s_bf], axis=0))
                ys[p].append(jnp.where(lane8[0:1] < DH_A, yy[0:1], yy[1:2]))
        for p in range(N_PAIR):
            y_ref[0, pl.ds(base, rows), p * LANES:(p + 1) * LANES] = jnp.concatenate(ys[p], axis=0)
        return carry

    lax.fori_loop(0, tc // rows, group, 0)

    @pl.when(t_blk == pl.num_programs(1) - 1)
    def _():
        s_out_ref[0] = s_ref[...]


def _rwkv_scan(r, w, k, v, kk, beta, s0, tc):
    b, t, _ = r.shape
    s0p = s0.astype(F32).reshape(b, N_PAIR, 2, DH_A, DH_A).transpose(0, 1, 3, 2, 4).reshape(b, N_PAIR, DH_A, LANES)
    seq = lambda: pl.BlockSpec((1, tc, D_A), lambda i, j: (i, j, 0))
    st = lambda: pl.BlockSpec((1, N_PAIR, DH_A, LANES), lambda i, j: (i, 0, 0, 0))
    y, s_fin = pl.pallas_call(
        functools.partial(_rwkv_scan_kernel, tc),
        grid=(b, t // tc),
        in_specs=[seq() for _ in range(6)] + [st(), pl.BlockSpec((LANES, LANES), lambda i, j: (0, 0))],
        out_specs=[seq(), st()],
        out_shape=[jax.ShapeDtypeStruct((b, t, D_A), F32),
                   jax.ShapeDtypeStruct((b, N_PAIR, DH_A, LANES), F32)],
        scratch_shapes=[pltpu.VMEM((N_PAIR, DH_A, LANES), F32)],
        compiler_params=_cparams("parallel", "arbitrary"),
    )(r, w, k, v, kk, beta, s0p, _head_ones())
    s_fin = s_fin.reshape(b, N_PAIR, DH_A, 2, DH_A).transpose(0, 1, 3, 2, 4).reshape(b, 2 * N_PAIR, DH_A, DH_A)
    return y, s_fin


def _rwkv_post_kernel(y_ref, bonus_ref, g_ref, gng_ref, gnb_ref, ones_ref, o_ref):
    ones = ones_ref[...]
    for j in range(D_A // LANES):
        sl = slice(j * LANES, (j + 1) * LANES)
        y = y_ref[:, sl]
        mean = _dot01(y, ones) * (1.0 / DH_A)
        d = y - mean
        var = _dot01(d * d, ones) * (1.0 / DH_A)
        yn = d * lax.rsqrt(var + GN_EPS) * gng_ref[:, sl] + gnb_ref[:, sl]
        o_ref[:, sl] = ((yn + bonus_ref[:, sl]) * g_ref[:, sl]).astype(o_ref.dtype)


def _rwkv_post(y, bonus, g, prm, tm):
    n = y.shape[0]
    ones = _head_ones()
    blk = lambda: pl.BlockSpec((tm, D_A), lambda i: (i, 0))
    vec = lambda: pl.BlockSpec((1, D_A), lambda i: (0, 0))
    return pl.pallas_call(
        _rwkv_post_kernel,
        grid=(n // tm,),
        in_specs=[blk(), blk(), blk(), vec(), vec(), pl.BlockSpec((LANES, LANES), lambda i: (0, 0))],
        out_specs=blk(),
        out_shape=jax.ShapeDtypeStruct((n, D_A), BF16),
        compiler_params=_cparams("parallel"),
    )(y, bonus, g, prm['gn_g'].reshape(1, D_A), prm['gn_b'].reshape(1, D_A), ones)


def _pool_kernel(tc, n_prev, u_ref, prev_ref, w_ref, scale_ref, o_ref, ext_ref):
    t_blk = pl.program_id(1)

    @pl.when(t_blk == 0)
    def _():
        ext_ref[pl.ds(0, POOL_HALO), :] = prev_ref[0]

    @pl.when(t_blk > 0)
    def _():
        ext_ref[pl.ds(0, POOL_HALO), :] = ext_ref[pl.ds(tc, POOL_HALO), :]

    u = u_ref[0]
    ext_ref[pl.ds(POOL_HALO, tc), :] = u
    pos = n_prev + t_blk * tc + lax.broadcasted_iota(jnp.int32, (tc, 1), 0) + 1
    for gi, win in enumerate(POOL_WINDOWS):
        sl = slice(gi * D_PG, (gi + 1) * D_PG)
        tot = u[:, sl]
        for j in range(1, win):
            tot = tot + ext_ref[pl.ds(POOL_HALO - j, tc), sl]
        cnt = jnp.minimum(pos, win).astype(F32)
        diff = tot / cnt - u[:, sl]
        o_ref[0, :, sl] = (_dot(diff.astype(BF16), w_ref[gi]) * scale_ref[:, sl]).astype(o_ref.dtype)


def _pool(u, u_prev, prm, tc):
    b, t, d_p = u.shape
    n_prev = u_prev.shape[1]
    halo = jnp.concatenate([jnp.zeros((b, POOL_HALO - n_prev, d_p), F32), u_prev.astype(F32)], axis=1)
    return pl.pallas_call(
        functools.partial(_pool_kernel, tc, n_prev),
        grid=(b, t // tc),
        in_specs=[pl.BlockSpec((1, tc, d_p), lambda i, j: (i, j, 0)),
                  pl.BlockSpec((1, POOL_HALO, d_p), lambda i, j: (i, 0, 0)),
                  pl.BlockSpec((len(POOL_WINDOWS), D_PG, D_PG), lambda i, j: (0, 0, 0)),
                  pl.BlockSpec((1, d_p), lambda i, j: (0, 0))],
        out_specs=pl.BlockSpec((1, tc, d_p), lambda i, j: (i, j, 0)),
        out_shape=jax.ShapeDtypeStruct((b, t, d_p), BF16),
        scratch_shapes=[pltpu.VMEM((tc + POOL_HALO, d_p), F32)],
        compiler_params=_cparams("parallel", "arbitrary"),
    )(u, halo, prm['w_pool'].astype(BF16), prm['pool_scale'].reshape(1, d_p))


Q_ROWS = 8
SUPER = 2 * PAGE


def _stick_logits(z, mask, upper):
    ls = _log_sigmoid(z)
    l1m = ls - z
    if mask is not None:
        l1m = jnp.where(mask, l1m, 0.0)
    return ls, _dot01(l1m, upper), jnp.sum(l1m, axis=1, keepdims=True)


def _stick_weights(ls, suffix, carry, mask):
    wts = jnp.exp(ls + suffix + carry)
    if mask is not None:
        wts = jnp.where(mask, wts, 0.0)
    return wts


def _stick_prompt_kernel(q_ref, k_ref, v_ref, bias_ref, upper_ref, o_ref):
    h = pl.program_id(1)
    i = pl.program_id(2)
    q = q_ref[0]
    bias = bias_ref[h]
    upper = upper_ref[...]
    q_pos = i * SUPER + lax.broadcasted_iota(jnp.int32, (SUPER, PAGE), 0)
    col = lax.broadcasted_iota(jnp.int32, (SUPER, PAGE), 1)

    def body(jj, state):
        carry, acc = state
        parts = []
        for sub in (1, 0):
            start = pl.multiple_of((i - jj) * SUPER + sub * PAGE, PAGE)
            kb = k_ref[0, pl.ds(start, PAGE), :].astype(BF16)
            vb = v_ref[0, pl.ds(start, PAGE), :].astype(BF16)
            mask = start + col < q_pos
            z = _dot_nt(q, kb) * (DH_C ** -0.5) + bias
            parts.append((vb, mask) + _stick_logits(z, mask, upper))
        for vb, mask, ls, suffix, total in parts:
            acc = acc + _dot(_stick_weights(ls, suffix, carry, mask).astype(BF16), vb)
            carry = carry + total
        return carry, acc

    _, acc = lax.fori_loop(0, i + 1, body, (jnp.zeros((SUPER, 1), F32), jnp.zeros((SUPER, DH_C), F32)))
    o_ref[0] = acc.astype(o_ref.dtype)


def _upper01():
    i = jnp.arange(PAGE)
    return (i[:, None] > i[None, :]).astype(BF16)


def _stick_prompt(q, k, v, bias):
    b, t, _ = q.shape
    return pl.pallas_call(
        _stick_prompt_kernel,
        grid=(b, H_C, t // SUPER),
        in_specs=[pl.BlockSpec((1, SUPER, DH_C), lambda bi, h, i: (bi, i, h)),
                  pl.BlockSpec((1, t, DH_C), lambda bi, h, i: (bi, 0, h)),
                  pl.BlockSpec((1, t, DH_C), lambda bi, h, i: (bi, 0, h)),
                  pl.BlockSpec(memory_space=pltpu.SMEM),
                  pl.BlockSpec((PAGE, PAGE), lambda bi, h, i: (0, 0))],
        out_specs=pl.BlockSpec((1, SUPER, DH_C), lambda bi, h, i: (bi, i, h)),
        out_shape=jax.ShapeDtypeStruct((b, t, H_C * DH_C), BF16),
        compiler_params=_cparams("parallel", "parallel", "arbitrary"),
    )(q, k, v, bias.astype(F32), _upper01())


def _stick_sample_kernel(n_q, pt_ref, q_ref, kn_ref, vn_ref, kc_ref, vc_ref, bias_ref, upper_ref, o_ref,
                         carry_ref, acc_ref):
    j = pl.program_id(1)
    upper = upper_ref[...]
    rows = H_C * Q_ROWS

    def visit(k_page, v_page, mask, carry):
        head = lambda page, h: page[pl.ds(h, PAGE, stride=H_C), :].astype(BF16)
        rows_of = lambda x, h: x[h * Q_ROWS:(h + 1) * Q_ROWS, :].astype(BF16)
        z = jnp.concatenate([_dot_nt(rows_of(q_ref[0], h), head(k_page, h)) for h in range(H_C)],
                            axis=0) * (DH_C ** -0.5) + bias_ref[...]
        ls, suffix, total = _stick_logits(z, mask, upper)
        wts = _stick_weights(ls, suffix, carry, mask)
        out = jnp.concatenate([_dot(rows_of(wts, h), head(v_page, h)) for h in range(H_C)], axis=0)
        return out, carry + total

    @pl.when(j == 0)
    def _():
        q_idx = lax.broadcasted_iota(jnp.int32, (rows, PAGE), 0) % Q_ROWS
        col = lax.broadcasted_iota(jnp.int32, (rows, PAGE), 1)
        out, carry = visit(kn_ref.at[0], vn_ref.at[0], col < q_idx, jnp.zeros((rows, 1), F32))
        acc_ref[...] = out
        carry_ref[...] = carry

    @pl.when(j > 0)
    def _():
        out, carry = visit(kc_ref.at[0, 0], vc_ref.at[0, 0], None, carry_ref[...])
        acc_ref[...] += out
        carry_ref[...] = carry

    @pl.when(j == pl.num_programs(1) - 1)
    def _():
        for h in range(H_C):
            o_ref[0, :, h * DH_C:(h + 1) * DH_C] = acc_ref[h * Q_ROWS:h * Q_ROWS + n_q, :].astype(o_ref.dtype)


def _stick_sample(q, k_new, v_new, cache_k, cache_v, layer, page_table, bias):
    b, n_q, width = k_new.shape
    n_pages = page_table.shape[1]
    depth, n_pool = cache_k.shape[:2]
    rows = H_C * Q_ROWS
    qh = q.reshape(b, n_q, H_C, DH_C).transpose(0, 2, 1, 3)
    qh = jnp.pad(qh, ((0, 0), (0, 0), (0, Q_ROWS - n_q), (0, 0))).reshape(b, rows, DH_C)
    as_page = lambda x: jnp.pad(x, ((0, 0), (0, PAGE - n_q), (0, 0))).reshape(b, PAGE * H_C, DH_C)
    bias_rows = jnp.repeat(bias.astype(F32), Q_ROWS).reshape(rows, 1)
    page = lambda bi, j, pt: (layer, pt[bi, n_pages - jnp.maximum(j, 1)], 0, 0)
    grid_spec = pltpu.PrefetchScalarGridSpec(
        num_scalar_prefetch=1,
        grid=(b, n_pages + 1),
        in_specs=[pl.BlockSpec((1, rows, DH_C), lambda bi, j, pt: (bi, 0, 0)),
                  pl.BlockSpec((1, PAGE * H_C, DH_C), lambda bi, j, pt: (bi, 0, 0)),
                  pl.BlockSpec((1, PAGE * H_C, DH_C), lambda bi, j, pt: (bi, 0, 0)),
                  pl.BlockSpec((1, 1, PAGE * H_C, DH_C), page),
                  pl.BlockSpec((1, 1, PAGE * H_C, DH_C), page),
                  pl.BlockSpec((rows, 1), lambda bi, j, pt: (0, 0)),
                  pl.BlockSpec((PAGE, PAGE), lambda bi, j, pt: (0, 0))],
        out_specs=pl.BlockSpec((1, n_q, width), lambda bi, j, pt: (bi, 0, 0)),
        scratch_shapes=[pltpu.VMEM((rows, 1), F32), pltpu.VMEM((rows, DH_C), F32)],
    )
    return pl.pallas_call(
        functools.partial(_stick_sample_kernel, n_q),
        grid_spec=grid_spec,
        out_shape=jax.ShapeDtypeStruct((b, n_q, width), BF16),
        compiler_params=_cparams("parallel", "arbitrary"),
    )(page_table, qh, as_page(k_new), as_page(v_new),
      cache_k.reshape(depth, n_pool, PAGE * H_C, DH_C), cache_v.reshape(depth, n_pool, PAGE * H_C, DH_C),
      bias_rows, _upper01())


def _top_rounds(x, row_f, n_rounds, payload=None):
    big = float(x.shape[0])
    vals, tags = [], []
    for _ in range(n_rounds):
        m = jnp.max(x, axis=0, keepdims=True)
        pos = jnp.min(jnp.where(x == m, row_f, big), axis=0, keepdims=True)
        hit = row_f == pos
        vals.append(m)
        tags.append(pos if payload is None else jnp.max(jnp.where(hit, payload, -1.0), axis=0, keepdims=True))
        x = jnp.where(hit, -jnp.inf, x)
    return jnp.concatenate(vals, axis=0), jnp.concatenate(tags, axis=0)


def _peer_route_kernel(q_ref, keys_ref, idx_ref, gate_ref):
    tb = q_ref.shape[0]
    n_cand = PEER_TOPK * PEER_TOPK
    row_k = lax.broadcasted_iota(jnp.int32, (N_KEYS, tb), 0).astype(F32)
    row_c = lax.broadcasted_iota(jnp.int32, (n_cand, tb), 0).astype(F32)
    for h in range(PEER_HEADS):
        tops = []
        for p in range(2):
            c0 = (2 * h + p) * N_KEYS
            q_hi, q_lo = _split_bf16(q_ref[:, c0:c0 + N_KEYS])
            k_hi, k_lo = _split_bf16(keys_ref[2 * h + p])
            s = _dot_nt(k_hi, q_hi) + (_dot_nt(k_hi, q_lo) + _dot_nt(k_lo, q_hi))
            tops.append(_top_rounds(s, row_k, PEER_TOPK))
        (v1, i1), (v2, i2) = tops
        cand = jnp.concatenate([v1[i:i + 1] + v2 for i in range(PEER_TOPK)], axis=0)
        cidx = jnp.concatenate([i1[i:i + 1] * float(N_KEYS) + i2 for i in range(PEER_TOPK)], axis=0)
        top, eidx = _top_rounds(cand, row_c, PEER_TOPK, payload=cidx)
        ex = jnp.exp(top - top[0:1])
        rows = slice(h * PEER_TOPK, (h + 1) * PEER_TOPK)
        idx_ref[rows, :] = eidx.astype(jnp.int32)
        gate_ref[rows, :] = ex / jnp.sum(ex, axis=0, keepdims=True)


def _peer_route(q, sub_keys, tb):
    n, width = q.shape
    keys = sub_keys.reshape(2 * PEER_HEADS, N_KEYS, N_KEYS)
    return pl.pallas_call(
        _peer_route_kernel,
        grid=(n // tb,),
        in_specs=[pl.BlockSpec((tb, width), lambda i: (i, 0)),
                  pl.BlockSpec(keys.shape, lambda i: (0, 0, 0))],
        out_specs=[pl.BlockSpec((N_GATHER, tb), lambda i: (0, i)) for _ in range(2)],
        out_shape=[jax.ShapeDtypeStruct((N_GATHER, n), jnp.int32), jax.ShapeDtypeStruct((N_GATHER, n), F32)],
        compiler_params=_cparams("parallel"),
    )(q, keys)


def _gelu_tanh(x):
    return 0.5 * x * (1.0 + jnp.tanh(0.7978845608028654 * (x + 0.044715 * x * x * x)))


def _peer_pack(peer_u, peer_v):
    bits = lambda a: lax.bitcast_convert_type(a.astype(BF16), jnp.uint16).astype(jnp.uint32)
    return (bits(peer_v) << 16) | bits(peer_u)


def _peer_expert_kernel(tb, per_row_gate, idx_ref, h_ref, gate_t_ref, x_ref, g2_ref, tab_hbm, o_ref, buf, sems):
    step = pl.program_id(0)
    n_groups = tb // PEER_GROUP
    rows = PEER_GROUP * N_GATHER
    d = h_ref.shape[1]
    lane_t = lax.broadcasted_iota(jnp.int32, gate_t_ref.shape, 1)
    row_g = lax.broadcasted_iota(jnp.int32, (PEER_GROUP, LANES), 0)

    def issue_token(row, i, slot):
        for j in range(N_GATHER):
            e = idx_ref[row, j]
            pltpu.make_async_copy(tab_hbm.at[pl.ds(e, 1), :], buf.at[slot, pl.ds(i * N_GATHER + j, 1), :],
                                  sems.at[slot]).start()

    def wait(slot):
        pltpu.make_async_copy(tab_hbm.at[pl.ds(0, rows), :], buf.at[slot], sems.at[slot]).wait()

    @pl.when(step == 0)
    def _():
        for i in range(PEER_GROUP):
            issue_token(i, i, 0)

    def mix_token(slot, i, base):
        t = base + i
        words = lambda c: buf[slot, i * N_GATHER:(i + 1) * N_GATHER, c * LANES:(c + 1) * LANES]
        lanes = lambda c: slice(c * LANES, (c + 1) * LANES)
        acc = jnp.zeros((N_GATHER, LANES), F32)
        for c in range(d // LANES):
            u = lax.bitcast_convert_type(words(c) << 16, F32)
            acc = acc + u * h_ref[pl.ds(base, PEER_GROUP), lanes(c)][i:i + 1, :]
        gate_col = jnp.sum(jnp.where(lane_t == t, gate_t_ref[...], 0.0), axis=1, keepdims=True)
        coef = gate_col * _gelu_tanh(jnp.sum(acc, axis=1, keepdims=True))
        for c in range(d // LANES):
            v = lax.bitcast_convert_type(words(c) & jnp.uint32(0xFFFF0000), F32)
            mix = jnp.where(row_g == i, jnp.sum(v * coef, axis=0, keepdims=True), 0.0)
            g2 = g2_ref[0, pl.ds(base, PEER_GROUP), lanes(c)] if per_row_gate else g2_ref[0, :, lanes(c)]
            tile = x_ref if i == 0 else o_ref
            o_ref[pl.ds(base, PEER_GROUP), lanes(c)] = tile[pl.ds(base, PEER_GROUP), lanes(c)] + g2 * mix

    def group_pair(gp, c):
        for slot in range(2):
            base = pl.multiple_of((2 * gp + slot) * PEER_GROUP, PEER_GROUP)
            wait(slot)
            for i in range(PEER_GROUP):
                issue_token(base + PEER_GROUP + i, i, 1 - slot)
                mix_token(slot, i, base)
        return c

    lax.fori_loop(0, n_groups // 2, group_pair, 0)

    @pl.when(step == pl.num_programs(0) - 1)
    def _():
        wait(0)


def _peer_expert(h2, idx_t, gate_t, x, g2, table, seq_rows, tb):
    n, d = h2.shape
    assert (tb // PEER_GROUP) % 2 == 0
    per_row_gate = g2.shape[1] != 1
    n_steps = n // tb
    idx3 = idx_t.T.reshape(n_steps, tb, N_GATHER)
    ahead = jnp.concatenate([idx3[1:, :PEER_GROUP], idx3[-1:, :PEER_GROUP]], axis=0)
    idx_ext = jnp.concatenate([idx3, ahead], axis=1).reshape(n_steps * (tb + PEER_GROUP), N_GATHER)
    blk = lambda w: pl.BlockSpec((tb, w), lambda i: (i, 0))
    return pl.pallas_call(
        functools.partial(_peer_expert_kernel, tb, per_row_gate),
        grid=(n_steps,),
        in_specs=[pl.BlockSpec((tb + PEER_GROUP, N_GATHER), lambda i: (i, 0), memory_space=pltpu.SMEM),
                  blk(d), pl.BlockSpec((N_GATHER, tb), lambda i: (0, i)), blk(d), _group_spec(g2, tb, seq_rows),
                  pl.BlockSpec(memory_space=pl.ANY)],
        out_specs=blk(d),
        out_shape=jax.ShapeDtypeStruct((n, d), F32),
        scratch_shapes=[pltpu.VMEM((2, PEER_GROUP * N_GATHER, d), jnp.uint32), pltpu.SemaphoreType.DMA((2,))],
        compiler_params=_cparams("arbitrary"),
    )(idx_ext, h2, gate_t, x, g2, table)


def _tile(n, pref):
    return pref if n % pref == 0 else n


def _layer(x, mods, prm, wts, shift_prev, wkv0, pool_prev, attend, seq_len):
    n, d = x.shape
    b = n // seq_len
    sh1, sc1, g1, sh2, sc2, g2 = mods
    tm = _tile(n, 512)
    h, = _norm(x, prm['norm1_g'], sc1, sh1, seq_len, _tile(n, 256), ('bf16',))
    proj = lambda w, dt, tn: _matmul([h], [w], [(0, 0)], dt, tm, tn)
    p_a = proj(wts['in_a'], F32, 256).reshape(b, seq_len, A_PROJ)
    u = proj(wts['in_u'], F32, 512).reshape(b, seq_len, -1)
    q_c = proj(wts['in_q'], BF16 if attend == 'prompt' else F32, 512).reshape(b, seq_len, -1)
    k_c = proj(wts['in_k'], F32, 512).reshape(b, seq_len, -1)
    v_c = proj(wts['in_v'], F32, 512).reshape(b, seq_len, -1)

    tc = _tile(seq_len, 256)
    r, w, k, v, kk, beta, g, bonus = _rwkv_prep(p_a, shift_prev, prm, tc)
    y, wkv_new = _rwkv_scan(r, w, k, v, kk, beta, wkv0, tc)
    flat = lambda a: a.reshape(n, -1)
    o_a = _rwkv_post(flat(y), flat(bonus), flat(g), prm, _tile(n, 512))
    o_b = flat(_pool(u, pool_prev, prm, tc))
    if attend == 'prompt':
        o_c = flat(_stick_prompt(q_c, k_c, v_c, prm['sb_bias']))
    else:
        o_c = flat(_stick_sample(q_c, k_c, v_c, *attend, prm['sb_bias']))

    d_a, d_p = o_a.shape[1], o_b.shape[1]
    w_o = wts['o']
    x = _matmul([o_a, o_b, o_c], [w_o[:d_a], w_o[d_a:d_a + d_p], w_o[d_a + d_p:]], [(0, 0), (1, 1), (2, 2)],
                F32, tm, 512, residual=(x, g1), seq_rows=seq_len)

    h2, h2_hi, h2_lo = _norm(x, prm['norm2_g'], sc2, sh2, seq_len, _tile(n, 256), ('f32', 'bf16', 'bf16'))
    q_p = _matmul([h2_hi, h2_lo], [wts['pq_hi'], wts['pq_lo']], [(0, 0), (1, 0), (0, 1)], F32, tm, 512)
    eidx, gate = _peer_route(q_p, prm['sub_keys'], _tile(n, 256))
    x = _peer_expert(h2, eidx, gate, x, g2, wts['peer'], seq_len, _tile(n, LANES))

    pool_ext = jnp.concatenate([pool_prev.astype(u.dtype), u], axis=1)
    state = (k_c.reshape(b, seq_len, H_C, DH_C), v_c.reshape(b, seq_len, H_C, DH_C), wkv_new,
             p_a[:, -1], pool_ext[:, pool_ext.shape[1] - (POOL_HALO - 1):])
    return x, state


def kernel(x_prompt, x_sample, cache_k, cache_v, state_wkv, state_shift, state_pool, page_table, c_prompt, c_sample, w_ada, b_ada, norm1_g, norm2_g, w_in, mu_shift, w0, w_w2, a0, w_a2, w_g2, k_k, k_a, r_k, gn_g, gn_b, w_pool, pool_scale, sb_bias, w_o, w_pq, sub_keys, peer_u, peer_v, final_g):
    depth = w_ada.shape[0]
    bp, seq, d = x_prompt.shape
    db, dseq, _ = x_sample.shape
    d_p = state_pool.shape[-1]
    d_c = H_C * DH_C

    c_all = jnp.concatenate([c_prompt, c_sample, jnp.zeros((-(bp + db) % 8, d), F32)], axis=0)
    mod_all = _ada(c_all, w_ada, b_ada).reshape(depth, c_all.shape[0], 6, d)

    xp = x_prompt.reshape(bp * seq, d)
    xs = x_sample.reshape(db * dseq, d)
    outs_p = [[] for _ in range(5)]
    outs_s = [[] for _ in range(5)]
    for l in range(depth):
        prm = {'norm1_g': norm1_g[l], 'norm2_g': norm2_g[l], 'mu_shift': mu_shift[l], 'w0': w0[l], 'w_w2': w_w2[l],
               'a0': a0[l], 'w_a2': w_a2[l], 'w_g2': w_g2[l], 'k_k': k_k[l], 'k_a': k_a[l], 'r_k': r_k[l].reshape(-1),
               'gn_g': gn_g[l], 'gn_b': gn_b[l], 'w_pool': w_pool[l], 'pool_scale': pool_scale[l],
               'sb_bias': sb_bias[l], 'sub_keys': sub_keys[l]}
        w_in_l = w_in[l]
        o_u, o_q = A_PROJ, A_PROJ + d_p
        pq_hi = w_pq[l].astype(BF16)
        wts = {'in_a': w_in_l[:, :o_u].astype(BF16), 'in_u': w_in_l[:, o_u:o_q].astype(BF16),
               'in_q': w_in_l[:, o_q:o_q + d_c].astype(BF16), 'in_k': w_in_l[:, o_q + d_c:o_q + 2 * d_c].astype(BF16),
               'in_v': w_in_l[:, o_q + 2 * d_c:].astype(BF16), 'o': w_o[l].astype(BF16),
               'pq_hi': pq_hi, 'pq_lo': (w_pq[l] - pq_hi.astype(F32)).astype(BF16),
               'peer': _peer_pack(peer_u[l], peer_v[l])}
        mods_p = [mod_all[l, :bp, i].reshape(bp, 1, d) for i in range(6)]
        mods_s = [jnp.repeat(mod_all[l, bp:bp + db, i], dseq, axis=0).reshape(1, db * dseq, d) for i in range(6)]
        xp, st_p = _layer(xp, mods_p, prm, wts, jnp.zeros((bp, A_PROJ), F32),
                          jnp.zeros((bp, 2 * N_PAIR, DH_A, DH_A), F32), jnp.zeros((bp, 0, d_p), F32), 'prompt', seq)
        xs, st_s = _layer(xs, mods_s, prm, wts, state_shift[l], state_wkv[l], state_pool[l],
                          (cache_k, cache_v, l, page_table), dseq)
        for i in range(5):
            outs_p[i].append(st_p[i])
            outs_s[i].append(st_s[i])
    y_prompt, = _norm(xp, final_g, None, None, seq, _tile(bp * seq, 256), ('f32',))
    y_sample, = _norm(xs, final_g, None, None, dseq, _tile(db * dseq, 256), ('f32',))
    stk = jnp.stack
    (kp, vp, wkvp, shp, poolp), (ks, vs, wkvs, shs, pools) = outs_p, outs_s
    return (y_prompt.reshape(bp, seq, d), y_sample.reshape(db, dseq, d), stk(kp), stk(vp), stk(ks), stk(vs),
            stk(wkvp), stk(wkvs), stk(shp), stk(shs), stk(poolp), stk(pools))
```

```python
import functools

import jax
import jax.numpy as jnp
from jax import lax
from jax.experimental import pallas as pl
from jax.experimental.pallas import tpu as pltpu

F32 = jnp.float32
BF16 = jnp.bfloat16

LANES = 128
VMEM_LIMIT = 56 * 1024 * 1024

DH_A = 64
N_PAIR = 8
D_A = 2 * DH_A * N_PAIR
R_LORA = 128
A_PROJ = 3 * D_A + 2 * R_LORA
GN_EPS = 64e-5
NORM_EPS = 1e-6
POOL_WINDOWS = (2, 4, 8, 16)
POOL_HALO = 16
D_PG = 256
DH_C = 128
H_C = 16
PAGE = 128
N_KEYS = 128
PEER_HEADS = 8
PEER_TOPK = 16
N_GATHER = PEER_HEADS * PEER_TOPK
PEER_GROUP = 8
FOLD = 8


def _cparams(*sem):
    return pltpu.CompilerParams(dimension_semantics=sem, vmem_limit_bytes=VMEM_LIMIT)


def _split_bf16(x):
    hi = x.astype(BF16)
    lo = (x - hi.astype(F32)).astype(BF16)
    return hi, lo


def _dot(a, b):
    return jnp.dot(a, b, preferred_element_type=F32)


def _dot_nt(a, b):
    return lax.dot_general(a, b, (((1,), (1,)), ((), ())), preferred_element_type=F32)


def _dot01(x, m01):
    hi, lo = _split_bf16(x)
    return _dot(hi, m01) + _dot(lo, m01)


def _head_ones():
    lane = jnp.arange(LANES)
    return (lane[:, None] // DH_A == lane[None, :] // DH_A).astype(BF16)


def _sigmoid(x):
    return 1.0 / (1.0 + jnp.exp(-x))


def _log_sigmoid(x):
    return jnp.minimum(x, 0.0) - jnp.log1p(jnp.exp(-jnp.abs(x)))


def _ada_kernel(c_ref, w_ref, b_ref, o_ref):
    c = c_ref[...]
    a = (c * _sigmoid(c)).astype(BF16)
    o_ref[0] = _dot(a, w_ref[0].astype(BF16)) + b_ref[0]


def _ada(c_all, w_ada, b_ada, tn=512):
    depth, d, n6 = w_ada.shape
    rows = c_all.shape[0]
    return pl.pallas_call(
        _ada_kernel,
        grid=(depth, n6 // tn),
        in_specs=[pl.BlockSpec((rows, d), lambda l, j: (0, 0)),
                  pl.BlockSpec((1, d, tn), lambda l, j: (l, 0, j)),
                  pl.BlockSpec((1, 1, tn), lambda l, j: (l, 0, j))],
        out_specs=pl.BlockSpec((1, rows, tn), lambda l, j: (l, 0, j)),
        out_shape=jax.ShapeDtypeStruct((depth, rows, n6), F32),
        compiler_params=_cparams("parallel", "parallel"),
    )(c_all, w_ada, b_ada.reshape(depth, 1, n6))


def _norm_kernel(modulate, n_out, *refs):
    if modulate:
        x_ref, g_ref, sc_ref, sh_ref = refs[:4]
        outs = refs[4:]
    else:
        x_ref, g_ref = refs[:2]
        outs = refs[2:]
    x = x_ref[...]
    y = x * lax.rsqrt(jnp.mean(x * x, axis=-1, keepdims=True) + NORM_EPS) * g_ref[...]
    if modulate:
        y = y * (1.0 + sc_ref[0]) + sh_ref[0]
    if n_out == 1:
        outs[0][...] = y.astype(outs[0].dtype)
    else:
        hi, lo = _split_bf16(y)
        outs[0][...] = y
        outs[1][...] = hi
        outs[2][...] = lo


def _group_spec(arr, tm, seq_rows, tn=None):
    g, r, d = arr.shape
    if r == 1:
        if tn is None:
            return pl.BlockSpec((1, 1, d), lambda i: (i * tm // seq_rows, 0, 0))
        return pl.BlockSpec((1, 1, tn), lambda i, j: (i * tm // seq_rows, 0, j))
    assert g == 1 and r == tm
    if tn is None:
        return pl.BlockSpec((1, tm, d), lambda i: (0, 0, 0))
    return pl.BlockSpec((1, tm, tn), lambda i, j: (0, 0, j))


def _norm(x, gain, sc, sh, seq_rows, tm, out_kinds):
    n, d = x.shape
    modulate = sc is not None
    in_specs = [pl.BlockSpec((tm, d), lambda i: (i, 0)), pl.BlockSpec((1, d), lambda i: (0, 0))]
    args = [x, gain.reshape(1, d)]
    if modulate:
        in_specs += [_group_spec(sc, tm, seq_rows), _group_spec(sh, tm, seq_rows)]
        args += [sc, sh]
    dts = [F32 if k == 'f32' else BF16 for k in out_kinds]
    outs = pl.pallas_call(
        functools.partial(_norm_kernel, modulate, len(out_kinds)),
        grid=(n // tm,),
        in_specs=in_specs,
        out_specs=[pl.BlockSpec((tm, d), lambda i: (i, 0)) for _ in dts],
        out_shape=[jax.ShapeDtypeStruct((n, d), dt) for dt in dts],
        compiler_params=_cparams("parallel"),
    )(*args)
    return outs


def _mm_kernel(n_a, n_b, pairs, residual, *refs):
    a_refs = refs[:n_a]
    b_refs = refs[n_a:n_a + n_b]
    rest = refs[n_a + n_b:]
    acc = None
    for ia, ib in pairs:
        d = _dot(a_refs[ia][...], b_refs[ib][...])
        acc = d if acc is None else acc + d
    if residual:
        x_ref, g_ref, o_ref = rest
        o_ref[...] = x_ref[...] + g_ref[0] * acc
    else:
        o_ref, = rest
        o_ref[...] = acc.astype(o_ref.dtype)


def _matmul(a_list, b_list, pairs, out_dtype, tm, tn, residual=None, seq_rows=None):
    m = a_list[0].shape[0]
    n = b_list[0].shape[1]
    in_specs = [pl.BlockSpec((tm, a.shape[1]), lambda i, j: (i, 0)) for a in a_list]
    in_specs += [pl.BlockSpec((b.shape[0], tn), lambda i, j: (0, j)) for b in b_list]
    args = list(a_list) + list(b_list)
    if residual is not None:
        x, gate = residual
        in_specs += [pl.BlockSpec((tm, tn), lambda i, j: (i, j)), _group_spec(gate, tm, seq_rows, tn)]
        args += [x, gate]
    return pl.pallas_call(
        functools.partial(_mm_kernel, len(a_list), len(b_list), tuple(pairs), residual is not None),
        grid=(m // tm, n // tn),
        in_specs=in_specs,
        out_specs=pl.BlockSpec((tm, tn), lambda i, j: (i, j)),
        out_shape=jax.ShapeDtypeStruct((m, n), out_dtype),
        compiler_params=_cparams("parallel", "parallel"),
    )(*args)


def _rwkv_prep_kernel(tc, p_ref, prev_ref, mu_ref, w0_ref, a0_ref, kk_ref, ka_ref, rk_ref,
                      ww_ref, wa_ref, wg_ref, ones_ref,
                      r_out, w_out, k_out, v_out, kk_out, b_out, g_out, bonus_out, ext_ref):
    t_blk = pl.program_id(1)

    @pl.when(t_blk == 0)
    def _():
        ext_ref[pl.ds(7, 1), :] = prev_ref[0]

    @pl.when(t_blk > 0)
    def _():
        ext_ref[pl.ds(7, 1), :] = ext_ref[pl.ds(7 + tc, 1), :]

    p = p_ref[0]
    ext_ref[pl.ds(8, tc), :] = p
    prev = ext_ref[pl.ds(7, tc), :]
    xs = p + (prev - p) * mu_ref[...]
    xr = xs[:, 0:D_A]
    xk = xs[:, D_A:2 * D_A]
    xv = xs[:, 2 * D_A:3 * D_A]
    x_wa = xs[:, 3 * D_A:3 * D_A + R_LORA]
    x_g = xs[:, 3 * D_A + R_LORA:]
    lw = _dot(jnp.tanh(x_wa).astype(BF16), ww_ref[...])
    la = _dot(x_wa.astype(BF16), wa_ref[...])
    g = _dot(_sigmoid(x_g).astype(BF16), wg_ref[...])
    w_log = -(jnp.maximum(-(w0_ref[...] + lw), 0.0) + jnp.log1p(jnp.exp(-jnp.abs(w0_ref[...] + lw)))) - 0.5
    decay = jnp.exp(-jnp.exp(w_log))
    a = _sigmoid(a0_ref[...] + la)
    kk = xk * kk_ref[...]
    k2 = xk * (1.0 + (a - 1.0) * ka_ref[...])
    rk = xr * k2 * rk_ref[...]
    ones = ones_ref[...]
    for j in range(D_A // LANES):
        sl = slice(j * LANES, (j + 1) * LANES)
        kkj = kk[:, sl]
        n2 = _dot01(kkj * kkj, ones)
        kkn = kkj / jnp.maximum(jnp.sqrt(n2), 1e-12)
        kk_out[0, :, sl] = kkn
        b_out[0, :, sl] = kkn * a[:, sl]
        bonus_out[0, :, sl] = _dot01(rk[:, sl], ones) * xv[:, sl]
    r_out[0] = xr
    w_out[0] = decay
    k_out[0] = k2
    v_out[0] = xv
    g_out[0] = g


def _rwkv_prep(p_a, shift_prev, prm, tc):
    b, t, width = p_a.shape
    row = lambda v: v.reshape(1, -1)
    zeros = jnp.zeros((DH_A, D_A), F32)
    ww = jnp.concatenate([prm['w_w2'], zeros], axis=0).astype(BF16)
    wa = jnp.concatenate([zeros, prm['w_a2']], axis=0).astype(BF16)
    ones = _head_ones()
    vec = lambda: pl.BlockSpec((1, D_A), lambda i, j: (0, 0))
    seq = lambda: pl.BlockSpec((1, tc, D_A), lambda i, j: (i, j, 0))
    outs = pl.pallas_call(
        functools.partial(_rwkv_prep_kernel, tc),
        grid=(b, t // tc),
        in_specs=[pl.BlockSpec((1, tc, width), lambda i, j: (i, j, 0)),
                  pl.BlockSpec((1, 1, width), lambda i, j: (i, 0, 0)),
                  pl.BlockSpec((1, width), lambda i, j: (0, 0)),
                  vec(), vec(), vec(), vec(), vec(),
                  pl.BlockSpec((R_LORA, D_A), lambda i, j: (0, 0)),
                  pl.BlockSpec((R_LORA, D_A), lambda i, j: (0, 0)),
                  pl.BlockSpec((R_LORA, D_A), lambda i, j: (0, 0)),
                  pl.BlockSpec((LANES, LANES), lambda i, j: (0, 0))],
        out_specs=[seq() for _ in range(8)],
        out_shape=[jax.ShapeDtypeStruct((b, t, D_A), F32) for _ in range(8)],
        scratch_shapes=[pltpu.VMEM((tc + 16, width), F32)],
        compiler_params=_cparams("parallel", "arbitrary"),
    )(p_a, shift_prev.reshape(b, 1, width), row(prm['mu_shift']), row(prm['w0']), row(prm['a0']),
      row(prm['k_k']), row(prm['k_a']), row(prm['r_k']), ww, wa, prm['w_g2'].astype(BF16), ones)
    return outs


def _rwkv_scan_kernel(tc, r_ref, w_ref, k_ref, v_ref, kk_ref, b_ref, s0_ref, ones_ref, y_ref, s_out_ref, s_ref):
    t_blk = pl.program_id(1)

    @pl.when(t_blk == 0)
    def _():
        s_ref[...] = s0_ref[0]

    lane = lax.broadcasted_iota(jnp.int32, (DH_A, LANES), 1)
    row = lax.broadcasted_iota(jnp.int32, (DH_A, LANES), 0)
    left = lane < DH_A
    diag = (lane % DH_A) == row
    lane8 = lax.broadcasted_iota(jnp.int32, (8, LANES), 1)
    row8 = lax.broadcasted_iota(jnp.int32, (8, LANES), 0)
    head_row = row8 == lane8 // DH_A
    ones = ones_ref[...]

    def seg_sum(x):
        s_l = jnp.sum(jnp.where(left, x, 0.0), axis=1, keepdims=True)
        s_r = jnp.sum(jnp.where(left, 0.0, x), axis=1, keepdims=True)
        return jnp.where(left, s_l, s_r)

    rows = min(tc, 8)

    def group(gi, carry):
        base = pl.multiple_of(gi * rows, rows)
        ops, v_cols = [], []
        for p in range(N_PAIR):
            sl = slice(p * LANES, (p + 1) * LANES)
            ops.append([ref[0, pl.ds(base, rows), sl] for ref in (r_ref, w_ref, k_ref, v_ref, kk_ref, b_ref)])
            v8 = ops[p][3]
            v_diag = jnp.concatenate([jnp.where(diag, v8[i:i + 1, :], 0.0) for i in range(rows)], axis=0)
            v_cols.append(_dot01(v_diag, ones))
        ys = [[] for _ in range(N_PAIR)]
        for i in range(rows):
            for p in range(N_PAIR):
                r8, w8, k8, _, kk8, b8 = (x[i:i + 1, :] for x in ops[p])
                s = s_ref[p]
                sa = seg_sum(s * kk8)
                s = s * w8 - sa * b8 + v_cols[p][i * DH_A:(i + 1) * DH_A, :] * k8
                s_ref[p] = s
                s_bf = s.astype(BF16)
                r2 = jnp.where(head_row, r8, 0.0).astype(BF16)
                yy = _dot_nt(r2, jnp.concatenate([s_bf, s_bf], axis=0))
                ys[p].append(jnp.where(lane8[0:1] < DH_A, yy[0:1], yy[1:2]))
        for p in range(N_PAIR):
            y_ref[0, pl.ds(base, rows), p * LANES:(p + 1) * LANES] = jnp.concatenate(ys[p], axis=0)
        return carry

    lax.fori_loop(0, tc // rows, group, 0)

    @pl.when(t_blk == pl.num_programs(1) - 1)
    def _():
        s_out_ref[0] = s_ref[...]


def _rwkv_scan(r, w, k, v, kk, beta, s0, tc):
    b, t, _ = r.shape
    s0p = s0.astype(F32).reshape(b, N_PAIR, 2, DH_A, DH_A).transpose(0, 1, 3, 2, 4).reshape(b, N_PAIR, DH_A, LANES)
    seq = lambda: pl.BlockSpec((1, tc, D_A), lambda i, j: (i, j, 0))
    st = lambda: pl.BlockSpec((1, N_PAIR, DH_A, LANES), lambda i, j: (i, 0, 0, 0))
    y, s_fin = pl.pallas_call(
        functools.partial(_rwkv_scan_kernel, tc),
        grid=(b, t // tc),
        in_specs=[seq() for _ in range(6)] + [st(), pl.BlockSpec((LANES, LANES), lambda i, j: (0, 0))],
        out_specs=[seq(), st()],
        out_shape=[jax.ShapeDtypeStruct((b, t, D_A), F32),
                   jax.ShapeDtypeStruct((b, N_PAIR, DH_A, LANES), F32)],
        scratch_shapes=[pltpu.VMEM((N_PAIR, DH_A, LANES), F32)],
        compiler_params=_cparams("parallel", "arbitrary"),
    )(r, w, k, v, kk, beta, s0p, _head_ones())
    s_fin = s_fin.reshape(b, N_PAIR, DH_A, 2, DH_A).transpose(0, 1, 3, 2, 4).reshape(b, 2 * N_PAIR, DH_A, DH_A)
    return y, s_fin


def _rwkv_post_kernel(y_ref, bonus_ref, g_ref, gng_ref, gnb_ref, ones_ref, o_ref):
    ones = ones_ref[...]
    for j in range(D_A // LANES):
        sl = slice(j * LANES, (j + 1) * LANES)
        y = y_ref[:, sl]
        mean = _dot01(y, ones) * (1.0 / DH_A)
        d = y - mean
        var = _dot01(d * d, ones) * (1.0 / DH_A)
        yn = d * lax.rsqrt(var + GN_EPS) * gng_ref[:, sl] + gnb_ref[:, sl]
        o_ref[:, sl] = ((yn + bonus_ref[:, sl]) * g_ref[:, sl]).astype(o_ref.dtype)


def _rwkv_post(y, bonus, g, prm, tm):
    n = y.shape[0]
    ones = _head_ones()
    blk = lambda: pl.BlockSpec((tm, D_A), lambda i: (i, 0))
    vec = lambda: pl.BlockSpec((1, D_A), lambda i: (0, 0))
    return pl.pallas_call(
        _rwkv_post_kernel,
        grid=(n // tm,),
        in_specs=[blk(), blk(), blk(), vec(), vec(), pl.BlockSpec((LANES, LANES), lambda i: (0, 0))],
        out_specs=blk(),
        out_shape=jax.ShapeDtypeStruct((n, D_A), BF16),
        compiler_params=_cparams("parallel"),
    )(y, bonus, g, prm['gn_g'].reshape(1, D_A), prm['gn_b'].reshape(1, D_A), ones)


def _pool_kernel(tc, n_prev, u_ref, prev_ref, w_ref, scale_ref, o_ref, ext_ref):
    t_blk = pl.program_id(1)

    @pl.when(t_blk == 0)
    def _():
        ext_ref[pl.ds(0, POOL_HALO), :] = prev_ref[0]

    @pl.when(t_blk > 0)
    def _():
        ext_ref[pl.ds(0, POOL_HALO), :] = ext_ref[pl.ds(tc, POOL_HALO), :]

    u = u_ref[0]
    ext_ref[pl.ds(POOL_HALO, tc), :] = u
    pos = n_prev + t_blk * tc + lax.broadcasted_iota(jnp.int32, (tc, 1), 0) + 1
    for gi, win in enumerate(POOL_WINDOWS):
        sl = slice(gi * D_PG, (gi + 1) * D_PG)
        tot = u[:, sl]
        for j in range(1, win):
            tot = tot + ext_ref[pl.ds(POOL_HALO - j, tc), sl]
        cnt = jnp.minimum(pos, win).astype(F32)
        diff = tot / cnt - u[:, sl]
        o_ref[0, :, sl] = (_dot(diff.astype(BF16), w_ref[gi]) * scale_ref[:, sl]).astype(o_ref.dtype)


def _pool(u, u_prev, prm, tc):
    b, t, d_p = u.shape
    n_prev = u_prev.shape[1]
    halo = jnp.concatenate([jnp.zeros((b, POOL_HALO - n_prev, d_p), F32), u_prev.astype(F32)], axis=1)
    return pl.pallas_call(
        functools.partial(_pool_kernel, tc, n_prev),
        grid=(b, t // tc),
        in_specs=[pl.BlockSpec((1, tc, d_p), lambda i, j: (i, j, 0)),
                  pl.BlockSpec((1, POOL_HALO, d_p), lambda i, j: (i, 0, 0)),
                  pl.BlockSpec((len(POOL_WINDOWS), D_PG, D_PG), lambda i, j: (0, 0, 0)),
                  pl.BlockSpec((1, d_p), lambda i, j: (0, 0))],
        out_specs=pl.BlockSpec((1, tc, d_p), lambda i, j: (i, j, 0)),
        out_shape=jax.ShapeDtypeStruct((b, t, d_p), BF16),
        scratch_shapes=[pltpu.VMEM((tc + POOL_HALO, d_p), F32)],
        compiler_params=_cparams("parallel", "arbitrary"),
    )(u, halo, prm['w_pool'].astype(BF16), prm['pool_scale'].reshape(1, d_p))


Q_ROWS = 8
SUPER = 2 * PAGE


def _stick_logits(z, mask, upper):
    ls = _log_sigmoid(z)
    l1m = ls - z
    if mask is not None:
        l1m = jnp.where(mask, l1m, 0.0)
    return ls, _dot01(l1m, upper), jnp.sum(l1m, axis=1, keepdims=True)


def _stick_weights(ls, suffix, carry, mask):
    wts = jnp.exp(ls + suffix + carry)
    if mask is not None:
        wts = jnp.where(mask, wts, 0.0)
    return wts


def _stick_prompt_kernel(q_ref, k_ref, v_ref, bias_ref, upper_ref, o_ref):
    h = pl.program_id(1)
    i = pl.program_id(2)
    q = q_ref[0]
    bias = bias_ref[h]
    upper = upper_ref[...]
    q_pos = i * SUPER + lax.broadcasted_iota(jnp.int32, (SUPER, PAGE), 0)
    col = lax.broadcasted_iota(jnp.int32, (SUPER, PAGE), 1)

    def body(jj, state):
        carry, acc = state
        parts = []
        for sub in (1, 0):
            start = pl.multiple_of((i - jj) * SUPER + sub * PAGE, PAGE)
            kb = k_ref[0, pl.ds(start, PAGE), :].astype(BF16)
            vb = v_ref[0, pl.ds(start, PAGE), :].astype(BF16)
            mask = start + col < q_pos
            z = _dot_nt(q, kb) * (DH_C ** -0.5) + bias
            parts.append((vb, mask) + _stick_logits(z, mask, upper))
        for vb, mask, ls, suffix, total in parts:
            acc = acc + _dot(_stick_weights(ls, suffix, carry, mask).astype(BF16), vb)
            carry = carry + total
        return carry, acc

    _, acc = lax.fori_loop(0, i + 1, body, (jnp.zeros((SUPER, 1), F32), jnp.zeros((SUPER, DH_C), F32)))
    o_ref[0] = acc.astype(o_ref.dtype)


def _upper01():
    i = jnp.arange(PAGE)
    return (i[:, None] > i[None, :]).astype(BF16)


def _stick_prompt(q, k, v, bias):
    b, t, _ = q.shape
    return pl.pallas_call(
        _stick_prompt_kernel,
        grid=(b, H_C, t // SUPER),
        in_specs=[pl.BlockSpec((1, SUPER, DH_C), lambda bi, h, i: (bi, i, h)),
                  pl.BlockSpec((1, t, DH_C), lambda bi, h, i: (bi, 0, h)),
                  pl.BlockSpec((1, t, DH_C), lambda bi, h, i: (bi, 0, h)),
                  pl.BlockSpec(memory_space=pltpu.SMEM),
                  pl.BlockSpec((PAGE, PAGE), lambda bi, h, i: (0, 0))],
        out_specs=pl.BlockSpec((1, SUPER, DH_C), lambda bi, h, i: (bi, i, h)),
        out_shape=jax.ShapeDtypeStruct((b, t, H_C * DH_C), BF16),
        compiler_params=_cparams("parallel", "parallel", "arbitrary"),
    )(q, k, v, bias.astype(F32), _upper01())


def _stick_sample_kernel(n_q, pt_ref, q_ref, kn_ref, vn_ref, kc_ref, vc_ref, bias_ref, upper_ref, o_ref,
                         carry_ref, acc_ref):
    j = pl.program_id(1)
    upper = upper_ref[...]
    rows = H_C * Q_ROWS

    def visit(k_page, v_page, mask, carry):
        head = lambda page, h: page[pl.ds(h, PAGE, stride=H_C), :].astype(BF16)
        rows_of = lambda x, h: x[h * Q_ROWS:(h + 1) * Q_ROWS, :].astype(BF16)
        z = jnp.concatenate([_dot_nt(rows_of(q_ref[0], h), head(k_page, h)) for h in range(H_C)],
                            axis=0) * (DH_C ** -0.5) + bias_ref[...]
        ls, suffix, total = _stick_logits(z, mask, upper)
        wts = _stick_weights(ls, suffix, carry, mask)
        out = jnp.concatenate([_dot(rows_of(wts, h), head(v_page, h)) for h in range(H_C)], axis=0)
        return out, carry + total

    @pl.when(j == 0)
    def _():
        q_idx = lax.broadcasted_iota(jnp.int32, (rows, PAGE), 0) % Q_ROWS
        col = lax.broadcasted_iota(jnp.int32, (rows, PAGE), 1)
        out, carry = visit(kn_ref.at[0], vn_ref.at[0], col < q_idx, jnp.zeros((rows, 1), F32))
        acc_ref[...] = out
        carry_ref[...] = carry

    @pl.when(j > 0)
    def _():
        out, carry = visit(kc_ref.at[0, 0], vc_ref.at[0, 0], None, carry_ref[...])
        acc_ref[...] += out
        carry_ref[...] = carry

    @pl.when(j == pl.num_programs(1) - 1)
    def _():
        for h in range(H_C):
            o_ref[0, :, h * DH_C:(h + 1) * DH_C] = acc_ref[h * Q_ROWS:h * Q_ROWS + n_q, :].astype(o_ref.dtype)


def _stick_sample(q, k_new, v_new, cache_k, cache_v, layer, page_table, bias):
    b, n_q, width = k_new.shape
    n_pages = page_table.shape[1]
    depth, n_pool = cache_k.shape[:2]
    rows = H_C * Q_ROWS
    qh = q.reshape(b, n_q, H_C, DH_C).transpose(0, 2, 1, 3)
    qh = jnp.pad(qh, ((0, 0), (0, 0), (0, Q_ROWS - n_q), (0, 0))).reshape(b, rows, DH_C)
    as_page = lambda x: jnp.pad(x, ((0, 0), (0, PAGE - n_q), (0, 0))).reshape(b, PAGE * H_C, DH_C)
    bias_rows = jnp.repeat(bias.astype(F32), Q_ROWS).reshape(rows, 1)
    page = lambda bi, j, pt: (layer, pt[bi, n_pages - jnp.maximum(j, 1)], 0, 0)
    grid_spec = pltpu.PrefetchScalarGridSpec(
        num_scalar_prefetch=1,
        grid=(b, n_pages + 1),
        in_specs=[pl.BlockSpec((1, rows, DH_C), lambda bi, j, pt: (bi, 0, 0)),
                  pl.BlockSpec((1, PAGE * H_C, DH_C), lambda bi, j, pt: (bi, 0, 0)),
                  pl.BlockSpec((1, PAGE * H_C, DH_C), lambda bi, j, pt: (bi, 0, 0)),
                  pl.BlockSpec((1, 1, PAGE * H_C, DH_C), page),
                  pl.BlockSpec((1, 1, PAGE * H_C, DH_C), page),
                  pl.BlockSpec((rows, 1), lambda bi, j, pt: (0, 0)),
                  pl.BlockSpec((PAGE, PAGE), lambda bi, j, pt: (0, 0))],
        out_specs=pl.BlockSpec((1, n_q, width), lambda bi, j, pt: (bi, 0, 0)),
        scratch_shapes=[pltpu.VMEM((rows, 1), F32), pltpu.VMEM((rows, DH_C), F32)],
    )
    return pl.pallas_call(
        functools.partial(_stick_sample_kernel, n_q),
        grid_spec=grid_spec,
        out_shape=jax.ShapeDtypeStruct((b, n_q, width), BF16),
        compiler_params=_cparams("parallel", "arbitrary"),
    )(page_table, qh, as_page(k_new), as_page(v_new),
      cache_k.reshape(depth, n_pool, PAGE * H_C, DH_C), cache_v.reshape(depth, n_pool, PAGE * H_C, DH_C),
      bias_rows, _upper01())


def _top_rounds(x, row_f, n_rounds, payload=None):
    big = float(x.shape[0])
    vals, tags = [], []
    for _ in range(n_rounds):
        m = jnp.max(x, axis=0, keepdims=True)
        pos = jnp.min(jnp.where(x == m, row_f, big), axis=0, keepdims=True)
        hit = row_f == pos
        vals.append(m)
        tags.append(pos if payload is None else jnp.max(jnp.where(hit, payload, -1.0), axis=0, keepdims=True))
        x = jnp.where(hit, -jnp.inf, x)
    return jnp.concatenate(vals, axis=0), jnp.concatenate(tags, axis=0)


def _peer_route_kernel(q_ref, keys_ref, idx_ref, gate_ref):
    tb = q_ref.shape[0]
    n_cand = PEER_TOPK * PEER_TOPK
    row_k = lax.broadcasted_iota(jnp.int32, (N_KEYS, tb), 0).astype(F32)
    row_c = lax.broadcasted_iota(jnp.int32, (n_cand, tb), 0).astype(F32)
    for h in range(PEER_HEADS):
        tops = []
        for p in range(2):
            c0 = (2 * h + p) * N_KEYS
            q_hi, q_lo = _split_bf16(q_ref[:, c0:c0 + N_KEYS])
            k_hi, k_lo = _split_bf16(keys_ref[2 * h + p])
            s = _dot_nt(k_hi, q_hi) + (_dot_nt(k_hi, q_lo) + _dot_nt(k_lo, q_hi))
            tops.append(_top_rounds(s, row_k, PEER_TOPK))
        (v1, i1), (v2, i2) = tops
        cand = jnp.concatenate([v1[i:i + 1] + v2 for i in range(PEER_TOPK)], axis=0)
        cidx = jnp.concatenate([i1[i:i + 1] * float(N_KEYS) + i2 for i in range(PEER_TOPK)], axis=0)
        top, eidx = _top_rounds(cand, row_c, PEER_TOPK, payload=cidx)
        ex = jnp.exp(top - top[0:1])
        rows = slice(h * PEER_TOPK, (h + 1) * PEER_TOPK)
        idx_ref[rows, :] = eidx.astype(jnp.int32)
        gate_ref[rows, :] = ex / jnp.sum(ex, axis=0, keepdims=True)


def _peer_route(q, sub_keys, tb):
    n, width = q.shape
    keys = sub_keys.reshape(2 * PEER_HEADS, N_KEYS, N_KEYS)
    return pl.pallas_call(
        _peer_route_kernel,
        grid=(n // tb,),
        in_specs=[pl.BlockSpec((tb, width), lambda i: (i, 0)),
                  pl.BlockSpec(keys.shape, lambda i: (0, 0, 0))],
        out_specs=[pl.BlockSpec((N_GATHER, tb), lambda i: (0, i)) for _ in range(2)],
        out_shape=[jax.ShapeDtypeStruct((N_GATHER, n), jnp.int32), jax.ShapeDtypeStruct((N_GATHER, n), F32)],
        compiler_params=_cparams("parallel"),
    )(q, keys)


def _gelu_tanh(x):
    return 0.5 * x * (1.0 + jnp.tanh(0.7978845608028654 * (x + 0.044715 * x * x * x)))


def _peer_pack(peer_u, peer_v):
    bits = lambda a: lax.bitcast_convert_type(a.astype(BF16), jnp.uint16).astype(jnp.uint32)
    words = (bits(peer_v) << 16) | bits(peer_u)
    return words.reshape(words.shape[0], FOLD, words.shape[1] // FOLD)


def _fold(a):
    return a.reshape(a.shape[:-1] + (FOLD, a.shape[-1] // FOLD))


def _peer_expert_kernel(tb, per_row_gate, idx_ref, h_ref, gate_ref, x_ref, g2_ref, tab_hbm, ones_ref, o_ref,
                        buf, sems):
    step = pl.program_id(0)
    n_groups = tb // PEER_GROUP
    rows = PEER_GROUP * N_GATHER
    n_chunk = h_ref.shape[2] // LANES
    lanes = lambda c: slice(c * LANES, (c + 1) * LANES)
    lane_e = lax.broadcasted_iota(jnp.int32, (1, N_GATHER), 1)
    eye = (lax.broadcasted_iota(jnp.int32, (N_GATHER, N_GATHER), 0)
           == lax.broadcasted_iota(jnp.int32, (N_GATHER, N_GATHER), 1))

    def issue_token(row, i, slot):
        for j in range(N_GATHER):
            e = idx_ref[row, j]
            pltpu.make_async_copy(tab_hbm.at[e], buf.at[slot, i * N_GATHER + j], sems.at[slot]).start()

    def wait(slot):
        pltpu.make_async_copy(tab_hbm.at[pl.ds(0, rows)], buf.at[slot], sems.at[slot]).wait()

    @pl.when(step == 0)
    def _():
        for i in range(PEER_GROUP):
            issue_token(i, i, 0)

    def mix_token(slot, i, base):
        t = base + i
        h = h_ref[t]
        scores = jnp.zeros((1, N_GATHER), F32)
        for j in range(N_GATHER):
            w = buf[slot, i * N_GATHER + j]
            p = None
            for c in range(n_chunk):
                u = lax.bitcast_convert_type(w[:, lanes(c)] << 16, F32)
                p = u * h[:, lanes(c)] if p is None else p + u * h[:, lanes(c)]
            tot = jnp.sum(jnp.sum(p, axis=1, keepdims=True), axis=0, keepdims=True)
            scores = jnp.where(lane_e == j, tot, scores)
        coef = gate_ref[pl.ds(base, PEER_GROUP), :][i:i + 1, :] * _gelu_tanh(scores)
        coef_rows = _dot01(jnp.where(eye, coef, 0.0), ones_ref[...])
        mix = [jnp.zeros((FOLD, LANES), F32) for _ in range(n_chunk)]
        for j in range(N_GATHER):
            w = buf[slot, i * N_GATHER + j]
            for c in range(n_chunk):
                v = lax.bitcast_convert_type(w[:, lanes(c)] & jnp.uint32(0xFFFF0000), F32)
                mix[c] = mix[c] + v * coef_rows[j:j + 1, :]
        g2 = g2_ref[0, t] if per_row_gate else g2_ref[0, 0]
        o_ref[t] = x_ref[t] + g2 * jnp.concatenate(mix, axis=1)

    def group_pair(gp, c):
        for slot in range(2):
            base = pl.multiple_of((2 * gp + slot) * PEER_GROUP, PEER_GROUP)
            wait(slot)
            for i in range(PEER_GROUP):
                issue_token(base + PEER_GROUP + i, i, 1 - slot)
                mix_token(slot, i, base)
        return c

    lax.fori_loop(0, n_groups // 2, group_pair, 0)

    @pl.when(step == pl.num_programs(0) - 1)
    def _():
        wait(0)


def _peer_expert(h2, idx_t, gate_t, x, g2, table, seq_rows, tb):
    n, d = h2.shape
    assert (tb // PEER_GROUP) % 2 == 0
    per_row_gate = g2.shape[1] != 1
    n_steps = n // tb
    idx3 = idx_t.T.reshape(n_steps, tb, N_GATHER)
    ahead = jnp.concatenate([idx3[1:, :PEER_GROUP], idx3[-1:, :PEER_GROUP]], axis=0)
    idx_ext = jnp.concatenate([idx3, ahead], axis=1).reshape(n_steps * (tb + PEER_GROUP), N_GATHER)
    width = d // FOLD
    blk = lambda: pl.BlockSpec((tb, FOLD, width), lambda i: (i, 0, 0))
    if per_row_gate:
        g2_spec = pl.BlockSpec((1, tb, FOLD, width), lambda i: (0, 0, 0, 0))
    else:
        g2_spec = pl.BlockSpec((1, 1, FOLD, width), lambda i: (i * tb // seq_rows, 0, 0, 0))
    out = pl.pallas_call(
        functools.partial(_peer_expert_kernel, tb, per_row_gate),
        grid=(n_steps,),
        in_specs=[pl.BlockSpec((tb + PEER_GROUP, N_GATHER), lambda i: (i, 0), memory_space=pltpu.SMEM),
                  blk(), pl.BlockSpec((tb, N_GATHER), lambda i: (i, 0)), blk(), g2_spec,
                  pl.BlockSpec(memory_space=pl.ANY), pl.BlockSpec((N_GATHER, LANES), lambda i: (0, 0))],
        out_specs=blk(),
        out_shape=jax.ShapeDtypeStruct((n, FOLD, width), F32),
        scratch_shapes=[pltpu.VMEM((2, PEER_GROUP * N_GATHER, FOLD, width), jnp.uint32),
                        pltpu.SemaphoreType.DMA((2,))],
        compiler_params=_cparams("arbitrary"),
    )(idx_ext, _fold(h2), gate_t.T, _fold(x), _fold(g2), table, jnp.ones((N_GATHER, LANES), BF16))
    return out.reshape(n, d)


def _tile(n, pref):
    return pref if n % pref == 0 else n


def _layer(x, mods, prm, wts, shift_prev, wkv0, pool_prev, attend, seq_len):
    n, d = x.shape
    b = n // seq_len
    sh1, sc1, g1, sh2, sc2, g2 = mods
    tm = _tile(n, 512)
    h, = _norm(x, prm['norm1_g'], sc1, sh1, seq_len, _tile(n, 256), ('bf16',))
    proj = lambda w, dt, tn: _matmul([h], [w], [(0, 0)], dt, tm, tn)
    p_a = proj(wts['in_a'], F32, 256).reshape(b, seq_len, A_PROJ)
    u = proj(wts['in_u'], F32, 512).reshape(b, seq_len, -1)
    q_c = proj(wts['in_q'], BF16 if attend == 'prompt' else F32, 512).reshape(b, seq_len, -1)
    k_c = proj(wts['in_k'], F32, 512).reshape(b, seq_len, -1)
    v_c = proj(wts['in_v'], F32, 512).reshape(b, seq_len, -1)

    tc = _tile(seq_len, 256)
    r, w, k, v, kk, beta, g, bonus = _rwkv_prep(p_a, shift_prev, prm, tc)
    y, wkv_new = _rwkv_scan(r, w, k, v, kk, beta, wkv0, tc)
    flat = lambda a: a.reshape(n, -1)
    o_a = _rwkv_post(flat(y), flat(bonus), flat(g), prm, _tile(n, 512))
    o_b = flat(_pool(u, pool_prev, prm, tc))
    if attend == 'prompt':
        o_c = flat(_stick_prompt(q_c, k_c, v_c, prm['sb_bias']))
    else:
        o_c = flat(_stick_sample(q_c, k_c, v_c, *attend, prm['sb_bias']))

    d_a, d_p = o_a.shape[1], o_b.shape[1]
    w_o = wts['o']
    x = _matmul([o_a, o_b, o_c], [w_o[:d_a], w_o[d_a:d_a + d_p], w_o[d_a + d_p:]], [(0, 0), (1, 1), (2, 2)],
                F32, tm, 512, residual=(x, g1), seq_rows=seq_len)

    h2, h2_hi, h2_lo = _norm(x, prm['norm2_g'], sc2, sh2, seq_len, _tile(n, 256), ('f32', 'bf16', 'bf16'))
    q_p = _matmul([h2_hi, h2_lo], [wts['pq_hi'], wts['pq_lo']], [(0, 0), (1, 0), (0, 1)], F32, tm, 512)
    eidx, gate = _peer_route(q_p, prm['sub_keys'], _tile(n, 256))
    x = _peer_expert(h2, eidx, gate, x, g2, wts['peer'], seq_len, _tile(n, LANES))

    pool_ext = jnp.concatenate([pool_prev.astype(u.dtype), u], axis=1)
    state = (k_c.reshape(b, seq_len, H_C, DH_C), v_c.reshape(b, seq_len, H_C, DH_C), wkv_new,
             p_a[:, -1], pool_ext[:, pool_ext.shape[1] - (POOL_HALO - 1):])
    return x, state


def kernel(x_prompt, x_sample, cache_k, cache_v, state_wkv, state_shift, state_pool, page_table, c_prompt, c_sample, w_ada, b_ada, norm1_g, norm2_g, w_in, mu_shift, w0, w_w2, a0, w_a2, w_g2, k_k, k_a, r_k, gn_g, gn_b, w_pool, pool_scale, sb_bias, w_o, w_pq, sub_keys, peer_u, peer_v, final_g):
    depth = w_ada.shape[0]
    bp, seq, d = x_prompt.shape
    db, dseq, _ = x_sample.shape
    d_p = state_pool.shape[-1]
    d_c = H_C * DH_C

    c_all = jnp.concatenate([c_prompt, c_sample, jnp.zeros((-(bp + db) % 8, d), F32)], axis=0)
    mod_all = _ada(c_all, w_ada, b_ada).reshape(depth, c_all.shape[0], 6, d)

    xp = x_prompt.reshape(bp * seq, d)
    xs = x_sample.reshape(db * dseq, d)
    outs_p = [[] for _ in range(5)]
    outs_s = [[] for _ in range(5)]
    for l in range(depth):
        prm = {'norm1_g': norm1_g[l], 'norm2_g': norm2_g[l], 'mu_shift': mu_shift[l], 'w0': w0[l], 'w_w2': w_w2[l],
               'a0': a0[l], 'w_a2': w_a2[l], 'w_g2': w_g2[l], 'k_k': k_k[l], 'k_a': k_a[l], 'r_k': r_k[l].reshape(-1),
               'gn_g': gn_g[l], 'gn_b': gn_b[l], 'w_pool': w_pool[l], 'pool_scale': pool_scale[l],
               'sb_bias': sb_bias[l], 'sub_keys': sub_keys[l]}
        w_in_l = w_in[l]
        o_u, o_q = A_PROJ, A_PROJ + d_p
        pq_hi = w_pq[l].astype(BF16)
        wts = {'in_a': w_in_l[:, :o_u].astype(BF16), 'in_u': w_in_l[:, o_u:o_q].astype(BF16),
               'in_q': w_in_l[:, o_q:o_q + d_c].astype(BF16), 'in_k': w_in_l[:, o_q + d_c:o_q + 2 * d_c].astype(BF16),
               'in_v': w_in_l[:, o_q + 2 * d_c:].astype(BF16), 'o': w_o[l].astype(BF16),
               'pq_hi': pq_hi, 'pq_lo': (w_pq[l] - pq_hi.astype(F32)).astype(BF16),
               'peer': _peer_pack(peer_u[l], peer_v[l])}
        mods_p = [mod_all[l, :bp, i].reshape(bp, 1, d) for i in range(6)]
        mods_s = [jnp.repeat(mod_all[l, bp:bp + db, i], dseq, axis=0).reshape(1, db * dseq, d) for i in range(6)]
        xp, st_p = _layer(xp, mods_p, prm, wts, jnp.zeros((bp, A_PROJ), F32),
                          jnp.zeros((bp, 2 * N_PAIR, DH_A, DH_A), F32), jnp.zeros((bp, 0, d_p), F32), 'prompt', seq)
        xs, st_s = _layer(xs, mods_s, prm, wts, state_shift[l], state_wkv[l], state_pool[l],
                          (cache_k, cache_v, l, page_table), dseq)
        for i in range(5):
            outs_p[i].append(st_p[i])
            outs_s[i].append(st_s[i])
    y_prompt, = _norm(xp, final_g, None, None, seq, _tile(bp * seq, 256), ('f32',))
    y_sample, = _norm(xs, final_g, None, None, dseq, _tile(db * dseq, 256), ('f32',))
    stk = jnp.stack
    (kp, vp, wkvp, shp, poolp), (ks, vs, wkvs, shs, pools) = outs_p, outs_s
    return (y_prompt.reshape(bp, seq, d), y_sample.reshape(db, dseq, d), stk(kp), stk(vp), stk(ks), stk(vs),
            stk(wkvp), stk(wkvs), stk(shp), stk(shs), stk(poolp), stk(pools))
```
